```python
import jax, jax.numpy as jnp
from jax import lax
import numpy as np

D_MODEL = 1024
BATCH = 4
SEQ = 8192
DEPTH = 2
DEC_BATCH = 128
DEC_SEQ = 4
PAST_LEN = 16384
PAGE_SIZE = 128

N_HEADS = 16
N_KV_HEADS = 4
HEAD_DIM = 64
GROUP = N_HEADS // N_KV_HEADS
WINDOW = 128
BLOCK = WINDOW
D_RNN = D_MODEL
N_LRU_BLOCKS = 4
LRU_BLOCK = D_RNN // N_LRU_BLOCKS
CONV_WIDTH = 4
LRU_C = 8.0
D_FF = -(-8 * D_MODEL // (3 * 256)) * 256
N_ATTN_LAYERS = (DEPTH + 1) // 2
N_REC_LAYERS = DEPTH // 2
EPS = 1e-6
NEG_INF = -1e30

kernel_name = "hybrid_swa_sink_rglru_decoder_step"


def rms_norm(x, g):
    xf = x.astype(jnp.float32)
    y = xf * lax.rsqrt(jnp.mean(xf * xf, axis=-1, keepdims=True) + EPS)
    return (y * g.astype(jnp.float32)).astype(x.dtype)


def alibi_slopes():
    return 2.0 ** (-8.0 * jnp.arange(1, N_HEADS + 1, dtype=jnp.float32) / N_HEADS)


def qkv_proj(h, w_qkv):
    B, T, _ = h.shape
    qkv = h @ w_qkv
    q = qkv[..., :N_HEADS * HEAD_DIM].reshape(B, T, N_KV_HEADS, GROUP, HEAD_DIM)
    k = qkv[..., N_HEADS * HEAD_DIM:(N_HEADS + N_KV_HEADS) * HEAD_DIM].reshape(B, T, N_KV_HEADS, HEAD_DIM)
    v = qkv[..., (N_HEADS + N_KV_HEADS) * HEAD_DIM:].reshape(B, T, N_KV_HEADS, HEAD_DIM)
    return q, k, v


def sink_attend(q, k, v, dist, valid, sinks):
    s = jnp.einsum('...qkgd,...skd->...kgqs', q, k, preferred_element_type=jnp.float32) * (HEAD_DIM ** -0.5)
    slopes = alibi_slopes().reshape(N_KV_HEADS, GROUP, 1, 1)
    s = jnp.where(valid, s - slopes * dist.astype(jnp.float32), NEG_INF)
    sink = jnp.broadcast_to(sinks.astype(jnp.float32).reshape(N_KV_HEADS, GROUP, 1, 1), s.shape[:-1] + (1,))
    p = jax.nn.softmax(jnp.concatenate([s, sink], axis=-1), axis=-1)[..., :-1]
    return jnp.einsum('...kgqs,...skd->...qkgd', p.astype(v.dtype), v)


def swa_prompt(h, w_qkv, w_o, sinks):
    B, S, _ = h.shape
    nb = S // BLOCK
    q, k, v = qkv_proj(h, w_qkv)
    qb = q.reshape(B, nb, BLOCK, N_KV_HEADS, GROUP, HEAD_DIM)
    kb = k.reshape(B, nb, BLOCK, N_KV_HEADS, HEAD_DIM)
    vb = v.reshape(B, nb, BLOCK, N_KV_HEADS, HEAD_DIM)
    zk = jnp.zeros_like(kb[:, :1])
    kk = jnp.concatenate([jnp.concatenate([zk, kb[:, :-1]], axis=1), kb], axis=2)
    vv = jnp.concatenate([jnp.concatenate([zk, vb[:, :-1]], axis=1), vb], axis=2)
    qi = jnp.arange(BLOCK)[:, None]
    si = jnp.arange(2 * BLOCK)[None, :]
    dist = qi - si + BLOCK
    band = (dist >= 0) & (dist <= WINDOW)
    key_pos = jnp.arange(nb)[:, None, None] * BLOCK - BLOCK + si
    valid = (band[None] & (key_pos >= 0))[:, None, None]
    o = sink_attend(qb, kk, vv, dist, valid, sinks)
    y = o.reshape(B, S, N_HEADS * HEAD_DIM) @ w_o
    return y, k[:, S - WINDOW:], v[:, S - WINDOW:]


def swa_sample(h, cache_k, cache_v, w_qkv, w_o, sinks):
    B, T, _ = h.shape
    q, k, v = qkv_proj(h, w_qkv)
    kk = jnp.concatenate([cache_k.astype(k.dtype), k], axis=1)
    vv = jnp.concatenate([cache_v.astype(v.dtype), v], axis=1)
    qi = jnp.arange(T)[:, None]
    si = jnp.arange(WINDOW + T)[None, :]
    dist = qi - si + WINDOW
    valid = (dist >= 0) & (dist <= WINDOW)
    o = sink_attend(q, kk, vv, dist, valid, sinks)
    y = o.reshape(B, T, N_HEADS * HEAD_DIM) @ w_o
    return y, kk[:, T:], vv[:, T:]


def lru_combine(left, right):
    a1, b1 = left
    a2, b2 = right
    return a1 * a2, a2 * b1 + b2


def recurrent_block(h, conv_state, h0, w_in, conv_w, conv_b, w_rg, b_rg, w_ig, b_ig, lam, w_out):
    B, T, _ = h.shape
    gate, xb = jnp.split(h @ w_in, 2, axis=-1)
    xp = jnp.concatenate([conv_state.astype(xb.dtype), xb], axis=1)
    xc = sum((xp[:, j:j + T] * conv_w[j] for j in range(CONV_WIDTH)), conv_b)
    xr = xc.reshape(B, T, N_LRU_BLOCKS, LRU_BLOCK)
    r = jax.nn.sigmoid(jnp.einsum('btnc,ncd->btnd', xr, w_rg) + b_rg.reshape(N_LRU_BLOCKS, LRU_BLOCK)).reshape(B, T, D_RNN)
    i = jax.nn.sigmoid(jnp.einsum('btnc,ncd->btnd', xr, w_ig) + b_ig.reshape(N_LRU_BLOCKS, LRU_BLOCK)).reshape(B, T, D_RNN)
    log_a = -LRU_C * r.astype(jnp.float32) * jax.nn.softplus(-lam.astype(jnp.float32))
    a = jnp.exp(log_a)
    b = jnp.sqrt(-jnp.expm1(2.0 * log_a)) * (i * xc).astype(jnp.float32)
    b = b.at[:, 0].add(a[:, 0] * h0.astype(jnp.float32))
    _, hs = lax.associative_scan(lru_combine, (a, b), axis=1)
    y = (hs.astype(h.dtype) * jax.nn.gelu(gate)) @ w_out
    return y, xp[:, T:], hs[:, -1]


def swiglu(h, w_in, w_out):
    g, u = jnp.split(h @ w_in, 2, axis=-1)
    return (jax.nn.silu(g) * u) @ w_out


def setup_inputs(seed: int = 0) -> dict:
    key = jax.random.key(seed)
    ks = jax.random.split(key, 24)
    f32 = jnp.float32
    nrm = lambda k, shape, s: jax.random.normal(k, shape, f32) * s
    qkv_w = (N_HEADS + 2 * N_KV_HEADS) * HEAD_DIM
    u = jax.random.uniform(ks[20], (N_REC_LAYERS, D_RNN), f32, minval=0.9, maxval=0.999)
    a_base = u ** (1.0 / LRU_C)
    return {
        "x_prompt": nrm(ks[0], (BATCH, SEQ, D_MODEL), 1.0),
        "x_sample": nrm(ks[1], (DEC_BATCH, DEC_SEQ, D_MODEL), 1.0),
        "cache_k": nrm(ks[2], (N_ATTN_LAYERS, DEC_BATCH, WINDOW, N_KV_HEADS, HEAD_DIM), 1.0),
        "cache_v": nrm(ks[3], (N_ATTN_LAYERS, DEC_BATCH, WINDOW, N_KV_HEADS, HEAD_DIM), 1.0),
        "state_conv": nrm(ks[4], (N_REC_LAYERS, DEC_BATCH, CONV_WIDTH - 1, D_RNN), 1.0),
        "state_h": nrm(ks[5], (N_REC_LAYERS, DEC_BATCH, D_RNN), 0.5),
        "attn_norm": 1.0 + nrm(ks[6], (N_ATTN_LAYERS, D_MODEL), 0.02),
        "w_qkv": nrm(ks[7], (N_ATTN_LAYERS, D_MODEL, qkv_w), D_MODEL ** -0.5),
        "w_attn_out": nrm(ks[8], (N_ATTN_LAYERS, N_HEADS * HEAD_DIM, D_MODEL), (N_HEADS * HEAD_DIM) ** -0.5),
        "attn_sinks": nrm(ks[9], (N_ATTN_LAYERS, N_HEADS), 1.0),
        "rec_norm": 1.0 + nrm(ks[10], (N_REC_LAYERS, D_MODEL), 0.02),
        "w_rec_in": nrm(ks[11], (N_REC_LAYERS, D_MODEL, 2 * D_RNN), D_MODEL ** -0.5),
        "conv_w": nrm(ks[12], (N_REC_LAYERS, CONV_WIDTH, D_RNN), CONV_WIDTH ** -0.5),
        "conv_b": nrm(ks[13], (N_REC_LAYERS, D_RNN), 0.1),
        "w_rgate": nrm(ks[14], (N_REC_LAYERS, N_LRU_BLOCKS, LRU_BLOCK, LRU_BLOCK), LRU_BLOCK ** -0.5),
        "b_rgate": nrm(ks[15], (N_REC_LAYERS, D_RNN), 0.1),
        "w_igate": nrm(ks[16], (N_REC_LAYERS, N_LRU_BLOCKS, LRU_BLOCK, LRU_BLOCK), LRU_BLOCK ** -0.5),
        "b_igate": nrm(ks[17], (N_REC_LAYERS, D_RNN), 0.1),
        "lru_lambda": jnp.log(a_base) - jnp.log1p(-a_base),
        "w_rec_out": nrm(ks[18], (N_REC_LAYERS, D_RNN, D_MODEL), D_RNN ** -0.5),
        "ffn_norm": 1.0 + nrm(ks[19], (DEPTH, D_MODEL), 0.02),
        "w_ffn_in": nrm(ks[21], (DEPTH, D_MODEL, 2 * D_FF), D_MODEL ** -0.5),
        "w_ffn_out": nrm(ks[22], (DEPTH, D_FF, D_MODEL), D_FF ** -0.5),
        "final_norm": 1.0 + nrm(ks[23], (D_MODEL,), 0.02),
    }


def reference(x_prompt, x_sample, cache_k, cache_v, state_conv, state_h, attn_norm, w_qkv, w_attn_out, attn_sinks, rec_norm, w_rec_in, conv_w, conv_b, w_rgate, b_rgate, w_igate, b_igate, lru_lambda, w_rec_out, ffn_norm, w_ffn_in, w_ffn_out, final_norm):
    xp, xs = x_prompt, x_sample
    nk_p, nv_p, nk_s, nv_s = [], [], [], []
    nc_p, nh_p, nc_s, nh_s = [], [], [], []
    for layer in range(DEPTH):
        j = layer // 2
        if layer % 2 == 0:
            yp, kp, vp = swa_prompt(rms_norm(xp, attn_norm[j]), w_qkv[j], w_attn_out[j], attn_sinks[j])
            ys, ks_, vs_ = swa_sample(rms_norm(xs, attn_norm[j]), cache_k[j], cache_v[j], w_qkv[j], w_attn_out[j], attn_sinks[j])
            nk_p.append(kp); nv_p.append(vp); nk_s.append(ks_); nv_s.append(vs_)
        else:
            rec_w = (w_rec_in[j], conv_w[j], conv_b[j], w_rgate[j], b_rgate[j], w_igate[j], b_igate[j], lru_lambda[j], w_rec_out[j])
            zc = jnp.zeros((xp.shape[0], CONV_WIDTH - 1, D_RNN), xp.dtype)
            zh = jnp.zeros((xp.shape[0], D_RNN), jnp.float32)
            yp, cp, hp = recurrent_block(rms_norm(xp, rec_norm[j]), zc, zh, *rec_w)
            ys, cs, hs = recurrent_block(rms_norm(xs, rec_norm[j]), state_conv[j], state_h[j], *rec_w)
            nc_p.append(cp); nh_p.append(hp); nc_s.append(cs); nh_s.append(hs)
        xp = xp + yp
        xs = xs + ys
        xp = xp + swiglu(rms_norm(xp, ffn_norm[layer]), w_ffn_in[layer], w_ffn_out[layer])
        xs = xs + swiglu(rms_norm(xs, ffn_norm[layer]), w_ffn_in[layer], w_ffn_out[layer])
    y_prompt = rms_norm(xp, final_norm)
    y_sample = rms_norm(xs, final_norm)
    return (y_prompt, y_sample, jnp.stack(nk_p), jnp.stack(nv_p), jnp.stack(nk_s), jnp.stack(nv_s), jnp.stack(nc_p), jnp.stack(nh_p), jnp.stack(nc_s), jnp.stack(nh_s))
```

```python
import functools

import jax
import jax.numpy as jnp
from jax import lax
from jax.experimental import pallas as pl
from jax.experimental.pallas import tpu as pltpu

EPS = 1e-6
NEG_INF = -1e30
LRU_C = 8.0
CONV_WIDTH = 4
N_LRU_BLOCKS = 4
BLOCK = 128
SUBLANES = 8
VMEM_LIMIT = 56 * 1024 * 1024

F32 = jnp.float32
BF16 = jnp.bfloat16


def _rms(x, g):
    return x * lax.rsqrt(jnp.mean(x * x, axis=-1, keepdims=True) + EPS) * g


def _dot(a, b):
    return jnp.dot(a, b, preferred_element_type=F32)


def _dot_nt(a, b):
    return lax.dot_general(a, b, (((1,), (1,)), ((), ())), preferred_element_type=F32)


def _softplus(z):
    return jnp.maximum(z, 0.0) + jnp.log1p(jnp.exp(-jnp.abs(z)))


def _full(shape):
    nd = len(shape)
    return pl.BlockSpec(shape, lambda *_: (0,) * nd)


def _params(sem):
    return pltpu.CompilerParams(dimension_semantics=sem, vmem_limit_bytes=VMEM_LIMIT)


def _attn_prompt_kernel(x_ref, g_ref, wqkv_ref, wo_ref, bias_ref, sink_ref,
                        y_ref, kout_ref, vout_ref,
                        q_scr, k_scr, v_scr, o_scr, *, tm, nh, nkv, hd):
    i = pl.program_id(1)
    last = pl.num_programs(1) - 1
    grp = nh // nkv
    nq, nk = nh * hd, nkv * hd

    x = x_ref[...]
    h = _rms(x, g_ref[...]).astype(BF16)
    qkv = _dot(h, wqkv_ref[...])
    q_scr[...] = (qkv[:, :nq] * (hd ** -0.5)).astype(BF16)
    k = qkv[:, nq:nq + nk]
    v = qkv[:, nq + nk:]

    @pl.when(i == 0)
    def _():
        k_scr[0:BLOCK, :] = jnp.zeros((BLOCK, nk), BF16)
        v_scr[0:BLOCK, :] = jnp.zeros((BLOCK, nk), BF16)

    k_scr[BLOCK:BLOCK + tm, :] = k.astype(BF16)
    v_scr[BLOCK:BLOCK + tm, :] = v.astype(BF16)

    @pl.when(i == last)
    def _():
        kout_ref[...] = k[tm - BLOCK:, :]
        vout_ref[...] = v[tm - BLOCK:, :]

    col = lax.broadcasted_iota(jnp.int32, (1, 2 * BLOCK), 1)
    first_pen = jnp.where((col < BLOCK) & (i == 0), NEG_INF, 0.0)

    for j in range(tm // BLOCK):
        r0 = j * BLOCK
        for kv in range(nkv):
            qg = jnp.concatenate(
                [q_scr[r0:r0 + BLOCK, (kv * grp + g) * hd:(kv * grp + g + 1) * hd]
                 for g in range(grp)], axis=0)
            kk = k_scr[r0:r0 + 2 * BLOCK, kv * hd:(kv + 1) * hd]
            vv = v_scr[r0:r0 + 2 * BLOCK, kv * hd:(kv + 1) * hd]
            s = _dot_nt(qg, kk) + bias_ref[kv]
            if j == 0:
                s = s + first_pen
            sink = sink_ref[kv]
            m = jnp.maximum(jnp.max(s, axis=-1, keepdims=True), sink)
            p = jnp.exp(s - m)
            denom = jnp.sum(p, axis=-1, keepdims=True) + jnp.exp(sink - m)
            o = _dot(p.astype(BF16), vv) * (1.0 / denom)
            for g in range(grp):
                c0 = (kv * grp + g) * hd
                o_scr[r0:r0 + BLOCK, c0:c0 + hd] = o[g * BLOCK:(g + 1) * BLOCK, :].astype(BF16)

    y_ref[...] = x + _dot(o_scr[...], wo_ref[...])

    k_scr[0:BLOCK, :] = k_scr[tm:tm + BLOCK, :]
    v_scr[0:BLOCK, :] = v_scr[tm:tm + BLOCK, :]


def _alibi_slopes(nh):
    return 2.0 ** (-8.0 * jnp.arange(1, nh + 1, dtype=F32) / nh)


def _attn_prompt(x, g, wqkv, wo, sinks, *, tm, nh, nkv, hd):
    bsz, seq, d = x.shape
    grp = nh // nkv
    nk = nkv * hd
    slopes = _alibi_slopes(nh)
    qi = jnp.arange(BLOCK)[:, None]
    si = jnp.arange(2 * BLOCK)[None, :]
    dist = qi - si + BLOCK
    band = (dist >= 0) & (dist <= BLOCK)
    bias = jnp.where(band[None], -slopes[:, None, None] * dist[None].astype(F32), NEG_INF)
    bias = bias.reshape(nkv, grp * BLOCK, 2 * BLOCK)
    sink_col = jnp.repeat(sinks.astype(F32), BLOCK).reshape(nkv, grp * BLOCK, 1)

    kern = functools.partial(_attn_prompt_kernel, tm=tm, nh=nh, nkv=nkv, hd=hd)
    return pl.pallas_call(
        kern,
        grid=(bsz, seq // tm),
        in_specs=[
            pl.BlockSpec((None, tm, d), lambda b, i: (b, i, 0)),
            _full((1, d)),
            _full(wqkv.shape),
            _full(wo.shape),
            _full(bias.shape),
            _full(sink_col.shape),
        ],
        out_specs=[
            pl.BlockSpec((None, tm, d), lambda b, i: (b, i, 0)),
            pl.BlockSpec((None, BLOCK, nk), lambda b, i: (b, 0, 0)),
            pl.BlockSpec((None, BLOCK, nk), lambda b, i: (b, 0, 0)),
        ],
        out_shape=[
            jax.ShapeDtypeStruct((bsz, seq, d), F32),
            jax.ShapeDtypeStruct((bsz, BLOCK, nk), F32),
            jax.ShapeDtypeStruct((bsz, BLOCK, nk), F32),
        ],
        scratch_shapes=[
            pltpu.VMEM((tm, nh * hd), BF16),
            pltpu.VMEM((tm + BLOCK, nk), BF16),
            pltpu.VMEM((tm + BLOCK, nk), BF16),
            pltpu.VMEM((tm, nh * hd), BF16),
        ],
        compiler_params=_params(("arbitrary", "arbitrary")),
        name="attn_prompt",
    )(x, g.reshape(1, d), wqkv, wo, bias, sink_col)


def _rms_matmul_kernel(x_ref, g_ref, w_ref, y_ref):
    h = _rms(x_ref[...], g_ref[...]).astype(BF16)
    y_ref[...] = _dot(h, w_ref[...])


def _rms_matmul(x, g, w):
    n, d = x.shape
    return pl.pallas_call(
        _rms_matmul_kernel,
        grid=(1,),
        in_specs=[_full((n, d)), _full((1, d)), _full(w.shape)],
        out_specs=_full((n, w.shape[1])),
        out_shape=jax.ShapeDtypeStruct((n, w.shape[1]), F32),
        compiler_params=_params(("arbitrary",)),
        name="rms_matmul",
    )(x, g.reshape(1, d), w)


def _attn_sample_kernel(q_ref, kn_ref, vn_ref, ck_ref, cv_ref, e_ref, et_ref, mask_ref,
                        biasc_ref, biasn_ref, sink_ref, o_ref, qx_scr, ox_scr,
                        *, bb, rows, t_new, hd):
    q = (q_ref[...].reshape(bb * rows, hd) * (hd ** -0.5)).astype(BF16)
    mask = mask_ref[...]
    qx_scr[...] = _dot(q, e_ref[...]) * mask
    sink = sink_ref[...]
    biasc = biasc_ref[...]
    biasn = biasn_ref[...]
    for b in range(bb):
        qx = qx_scr[b * rows:(b + 1) * rows, :]
        s_c = _dot_nt(qx.astype(BF16), ck_ref[b].astype(BF16)) + biasc
        kn = kn_ref[b]
        vn = vn_ref[b]
        s_n = [jnp.sum(qx * kn[j:j + 1, :], axis=-1, keepdims=True) + biasn[:, j:j + 1]
               for j in range(t_new)]
        m = jnp.maximum(jnp.max(s_c, axis=-1, keepdims=True), sink)
        for s in s_n:
            m = jnp.maximum(m, s)
        p_c = jnp.exp(s_c - m)
        p_n = [jnp.exp(s - m) for s in s_n]
        denom = jnp.sum(p_c, axis=-1, keepdims=True) + jnp.exp(sink - m)
        for p in p_n:
            denom = denom + p
        o = _dot(p_c.astype(BF16), cv_ref[b].astype(BF16))
        for j in range(t_new):
            o = o + p_n[j] * vn[j:j + 1, :]
        ox_scr[b * rows:(b + 1) * rows, :] = (o * (1.0 / denom) * mask[b * rows:(b + 1) * rows, :]).astype(BF16)
    o_ref[...] = _dot(ox_scr[...], et_ref[...]).reshape(bb, rows, hd)


def _attn_sample(q3, kn3, vn3, ck, cv, sinks, *, nh, nkv, hd, bb):
    dbsz, rows, _ = q3.shape
    t_new = kn3.shape[1]
    win = ck.shape[1]
    nk = nkv * hd
    grp = nh // nkv
    slopes = _alibi_slopes(nh)
    row_t = jnp.arange(rows) // nh
    row_h = jnp.arange(rows) % nh
    row_slope = slopes[row_h][:, None]
    sc = jnp.arange(win)[None, :]
    dist_c = (row_t[:, None] - sc + win).astype(F32)
    biasc = jnp.where(sc >= row_t[:, None], -row_slope * dist_c, NEG_INF)
    sn = jnp.arange(t_new)[None, :]
    dist_n = (row_t[:, None] - sn).astype(F32)
    biasn = jnp.where(sn <= row_t[:, None], -row_slope * dist_n, NEG_INF)
    sink_col = sinks.astype(F32)[row_h][:, None]
    lane_kv = jnp.arange(nk)[None, :] // hd
    mask = (lane_kv == (row_h // grp)[:, None]).astype(F32)
    mask = jnp.tile(mask, (bb, 1))
    e = jnp.tile(jnp.eye(hd, dtype=BF16), (1, nkv))
    et = e.T

    kern = functools.partial(_attn_sample_kernel, bb=bb, rows=rows, t_new=t_new, hd=hd)
    return pl.pallas_call(
        kern,
        grid=(dbsz // bb,),
        in_specs=[
            pl.BlockSpec((bb, rows, hd), lambda i: (i, 0, 0)),
            pl.BlockSpec((bb, t_new, nk), lambda i: (i, 0, 0)),
            pl.BlockSpec((bb, t_new, nk), lambda i: (i, 0, 0)),
            pl.BlockSpec((bb, win, nk), lambda i: (i, 0, 0)),
            pl.BlockSpec((bb, win, nk), lambda i: (i, 0, 0)),
            _full(e.shape), _full(et.shape), _full(mask.shape),
            _full(biasc.shape), _full(biasn.shape), _full(sink_col.shape),
        ],
        out_specs=pl.BlockSpec((bb, rows, hd), lambda i: (i, 0, 0)),
        out_shape=jax.ShapeDtypeStruct((dbsz, rows, hd), F32),
        scratch_shapes=[pltpu.VMEM((bb * rows, nk), F32), pltpu.VMEM((bb * rows, nk), BF16)],
        compiler_params=_params(("arbitrary",)),
        name="attn_sample",
    )(q3, kn3, vn3, ck, cv, e, et, mask, biasc, biasn, sink_col)


def _ffn_kernel(*refs, dff, chunk, proj, final):
    refs = list(refs)
    x_ref = refs.pop(0)
    if proj:
        o_ref, wo_ref = refs.pop(0), refs.pop(0)
    g_ref, win_ref, wout_ref = refs.pop(0), refs.pop(0), refs.pop(0)
    if final:
        gf_ref = refs.pop(0)
    y_ref, a_scr = refs

    x = x_ref[...]
    if proj:
        x = x + _dot(o_ref[...].astype(BF16), wo_ref[...])
    h = _rms(x, g_ref[...]).astype(BF16)
    for c in range(dff // chunk):
        gate = _dot(h, win_ref[:, c * chunk:(c + 1) * chunk])
        up = _dot(h, win_ref[:, dff + c * chunk:dff + (c + 1) * chunk])
        a_scr[:, c * chunk:(c + 1) * chunk] = (gate * jax.nn.sigmoid(gate) * up).astype(BF16)
    y = x + _dot(a_scr[...], wout_ref[...])
    if final:
        y = _rms(y, gf_ref[...])
    y_ref[...] = y


def _ffn(x, g, win, wout, *, tm, o=None, wo=None, gf=None, chunk=256):
    n, d = x.shape
    dff = wout.shape[0]
    proj, final = o is not None, gf is not None
    tile = pl.BlockSpec((tm, d), lambda i: (i, 0))
    args, specs = [x], [tile]
    if proj:
        args += [o, wo]
        specs += [pl.BlockSpec((tm, o.shape[1]), lambda i: (i, 0)), _full(wo.shape)]
    args += [g.reshape(1, d), win, wout]
    specs += [_full((1, d)), _full(win.shape), _full(wout.shape)]
    if final:
        args.append(gf.reshape(1, d))
        specs.append(_full((1, d)))
    kern = functools.partial(_ffn_kernel, dff=dff, chunk=chunk, proj=proj, final=final)
    return pl.pallas_call(
        kern,
        grid=(n // tm,),
        in_specs=specs,
        out_specs=tile,
        out_shape=jax.ShapeDtypeStruct((n, d), F32),
        scratch_shapes=[pltpu.VMEM((tm, dff), BF16)],
        compiler_params=_params(("arbitrary",)),
        name="ffn",
    )(*args)


def _lru_coeffs(xc, wrg_ref, brg_ref, wig_ref, big_ref, lam_ref, a_out, b_out):
    d = xc.shape[1]
    blk = d // N_LRU_BLOCKS
    for n in range(N_LRU_BLOCKS):
        sl = slice(n * blk, (n + 1) * blk)
        xcn = xc[:, sl]
        xcb = xcn.astype(BF16)
        r = jax.nn.sigmoid(_dot(xcb, wrg_ref[n]) + brg_ref[:, sl])
        ig = jax.nn.sigmoid(_dot(xcb, wig_ref[n]) + big_ref[:, sl])
        log_a = -LRU_C * r * _softplus(-lam_ref[:, sl])
        a = jnp.exp(log_a)
        a_out[:, sl] = a
        b_out[:, sl] = jnp.sqrt(-jnp.tanh(log_a) * (a * a + 1.0)) * (ig * xcn)


def _rec_prompt_kernel(x_ref, g_ref, win_ref, cw_ref, cb_ref, wrg_ref, brg_ref, wig_ref, big_ref,
                       lam_ref, wout_ref, y_ref, conv_ref, hout_ref,
                       xb_scr, gate_scr, a_scr, b_scr, a3_scr, b3_scr, hs_scr, carry_scr, *, tm):
    i = pl.program_id(1)
    last = pl.num_programs(1) - 1
    d = x_ref.shape[-1]
    ngrp = tm // SUBLANES

    @pl.when(i == 0)
    def _():
        xb_scr[0:SUBLANES, :] = jnp.zeros((SUBLANES, d), F32)
        carry_scr[...] = jnp.zeros((1, d), F32)

    x = x_ref[...]
    h = _rms(x, g_ref[...]).astype(BF16)
    gx = _dot(h, win_ref[...])
    gate_scr[...] = jax.nn.gelu(gx[:, :d])
    xb_scr[SUBLANES:SUBLANES + tm, :] = gx[:, d:]

    xc = cb_ref[...]
    for j in range(CONV_WIDTH):
        off = SUBLANES - (CONV_WIDTH - 1) + j
        xc = xc + xb_scr[off:off + tm, :] * cw_ref[j:j + 1, :]

    _lru_coeffs(xc, wrg_ref, brg_ref, wig_ref, big_ref, lam_ref, a_scr, b_scr)

    a = a_scr[...].reshape(ngrp, SUBLANES, d)
    b = b_scr[...].reshape(ngrp, SUBLANES, d)
    row = lax.broadcasted_iota(jnp.int32, (1, SUBLANES, 1), 1)
    shift = 1
    while shift < SUBLANES:
        keep = row >= shift
        a_prev = pltpu.roll(a, shift, axis=1)
        b_prev = pltpu.roll(b, shift, axis=1)
        b = jnp.where(keep, a * b_prev + b, b)
        a = jnp.where(keep, a * a_prev, a)
        shift *= 2
    a3_scr[...] = a
    b3_scr[...] = b

    def body(gidx, carry):
        hg = a3_scr[gidx] * carry + b3_scr[gidx]
        hs_scr[gidx] = hg
        return hg[SUBLANES - 1:SUBLANES, :]

    carry = lax.fori_loop(0, ngrp, body, carry_scr[...])
    carry_scr[...] = carry

    hs = hs_scr[...].reshape(tm, d)
    y_ref[...] = x + _dot((hs * gate_scr[...]).astype(BF16), wout_ref[...])

    @pl.when(i == last)
    def _():
        conv_ref[...] = xb_scr[SUBLANES + tm - (CONV_WIDTH - 1):SUBLANES + tm, :]
        hout_ref[...] = carry

    xb_scr[0:SUBLANES, :] = xb_scr[tm:tm + SUBLANES, :]


def _rec_weight_specs(d, win, wrg, wig, wout):
    return [_full((1, d)), _full(win.shape), _full((CONV_WIDTH, d)), _full((1, d)),
            _full(wrg.shape), _full((1, d)), _full(wig.shape), _full((1, d)), _full((1, d)),
            _full(wout.shape)]


def _rec_prompt(x, g, win, cw, cb, wrg, brg, wig, big, lam, wout, *, tm):
    bsz, seq, d = x.shape
    kern = functools.partial(_rec_prompt_kernel, tm=tm)
    ngrp = tm // SUBLANES
    return pl.pallas_call(
        kern,
        grid=(bsz, seq // tm),
        in_specs=[pl.BlockSpec((None, tm, d), lambda b, i: (b, i, 0))]
        + _rec_weight_specs(d, win, wrg, wig, wout),
        out_specs=[
            pl.BlockSpec((None, tm, d), lambda b, i: (b, i, 0)),
            pl.BlockSpec((None, CONV_WIDTH - 1, d), lambda b, i: (b, 0, 0)),
            pl.BlockSpec((None, 1, d), lambda b, i: (b, 0, 0)),
        ],
        out_shape=[
            jax.ShapeDtypeStruct((bsz, seq, d), F32),
            jax.ShapeDtypeStruct((bsz, CONV_WIDTH - 1, d), F32),
            jax.ShapeDtypeStruct((bsz, 1, d), F32),
        ],
        scratch_shapes=[
            pltpu.VMEM((tm + SUBLANES, d), F32),
            pltpu.VMEM((tm, d), F32),
            pltpu.VMEM((tm, d), F32),
            pltpu.VMEM((tm, d), F32),
            pltpu.VMEM((ngrp, SUBLANES, d), F32),
            pltpu.VMEM((ngrp, SUBLANES, d), F32),
            pltpu.VMEM((ngrp, SUBLANES, d), F32),
            pltpu.VMEM((1, d), F32),
        ],
        compiler_params=_params(("arbitrary", "arbitrary")),
        name="rec_prompt",
    )(x, g.reshape(1, d), win, cw, cb.reshape(1, d), wrg, brg.reshape(1, d), wig,
      big.reshape(1, d), lam.reshape(1, d), wout)


def _rec_sample_kernel(x_ref, sc_ref, h0_ref, g_ref, win_ref, cw_ref, cb_ref, wrg_ref, brg_ref,
                       wig_ref, big_ref, lam_ref, wout_ref, y_ref, conv_ref, hout_ref,
                       a_scr, b_scr, hs_scr, *, t_new, nb):
    d = x_ref.shape[-1]
    x = x_ref[...]
    h = _rms(x, g_ref[...]).astype(BF16)
    gx = _dot(h, win_ref[...])
    gate = gx[:, :d]
    xb = gx[:, d:]
    xp = [sc_ref[j] for j in range(CONV_WIDTH - 1)] + [xb[t * nb:(t + 1) * nb, :] for t in range(t_new)]
    xcs = []
    for t in range(t_new):
        acc = cb_ref[...]
        for j in range(CONV_WIDTH):
            acc = acc + xp[t + j] * cw_ref[j:j + 1, :]
        xcs.append(acc)
    xc = jnp.concatenate(xcs, axis=0)

    _lru_coeffs(xc, wrg_ref, brg_ref, wig_ref, big_ref, lam_ref, a_scr, b_scr)

    hprev = h0_ref[...]
    for t in range(t_new):
        sl = slice(t * nb, (t + 1) * nb)
        hprev = a_scr[sl, :] * hprev + b_scr[sl, :]
        hs_scr[sl, :] = hprev
    y_ref[...] = x + _dot((hs_scr[...] * jax.nn.gelu(gate)).astype(BF16), wout_ref[...])
    for j in range(CONV_WIDTH - 1):
        conv_ref[j] = xp[t_new + j]
    hout_ref[...] = hprev


def _rec_sample(x, sc, h0, g, win, cw, cb, wrg, brg, wig, big, lam, wout, *, t_new):
    n, d = x.shape
    nb = n // t_new
    kern = functools.partial(_rec_sample_kernel, t_new=t_new, nb=nb)
    return pl.pallas_call(
        kern,
        grid=(1,),
        in_specs=[_full((n, d)), _full(sc.shape), _full(h0.shape)]
        + _rec_weight_specs(d, win, wrg, wig, wout),
        out_specs=[_full((n, d)), _full(sc.shape), _full(h0.shape)],
        out_shape=[
            jax.ShapeDtypeStruct((n, d), F32),
            jax.ShapeDtypeStruct(sc.shape, F32),
            jax.ShapeDtypeStruct(h0.shape, F32),
        ],
        scratch_shapes=[pltpu.VMEM((n, d), F32)] * 3,
        compiler_params=_params(("arbitrary",)),
        name="rec_sample",
    )(x, sc, h0, g.reshape(1, d), win, cw, cb.reshape(1, d), wrg, brg.reshape(1, d), wig,
      big.reshape(1, d), lam.reshape(1, d), wout)


def _pick_tile(seq, want):
    tm = min(want, seq)
    while seq % tm:
        tm //= 2
    return tm


def kernel(x_prompt, x_sample, cache_k, cache_v, state_conv, state_h, attn_norm, w_qkv, w_attn_out,
           attn_sinks, rec_norm, w_rec_in, conv_w, conv_b, w_rgate, b_rgate, w_igate, b_igate,
           lru_lambda, w_rec_out, ffn_norm, w_ffn_in, w_ffn_out, final_norm, *, tile=512):
    bsz, seq, d = x_prompt.shape
    dbsz, t_new, _ = x_sample.shape
    depth = ffn_norm.shape[0]
    nh = attn_sinks.shape[1]
    hd = w_attn_out.shape[1] // nh
    nkv = (w_qkv.shape[2] // hd - nh) // 2
    nq, nk = nh * hd, nkv * hd
    win = cache_k.shape[2]
    tm = _pick_tile(seq, tile)
    tm_s = dbsz * t_new
    bb = _pick_tile(dbsz, 8)

    xp = x_prompt
    xs = x_sample.reshape(dbsz * t_new, d)
    time_major = False
    nk_p, nv_p, nk_s, nv_s, nc_p, nh_p, nc_s, nh_s = [], [], [], [], [], [], [], []

    for layer in range(depth):
        j = layer // 2
        wfi = w_ffn_in[layer].astype(BF16)
        wfo = w_ffn_out[layer].astype(BF16)
        gf = final_norm if layer == depth - 1 else None
        if layer % 2 == 0:
            wqkv = w_qkv[j].astype(BF16)
            wo = w_attn_out[j].astype(BF16)
            xp, kp, vp = _attn_prompt(xp, attn_norm[j], wqkv, wo, attn_sinks[j],
                                      tm=tm, nh=nh, nkv=nkv, hd=hd)
            nk_p.append(kp.reshape(bsz, BLOCK, nkv, hd))
            nv_p.append(vp.reshape(bsz, BLOCK, nkv, hd))

            if time_major:
                xs = xs.reshape(t_new, dbsz, d).transpose(1, 0, 2).reshape(dbsz * t_new, d)
                time_major = False
            qkv = _rms_matmul(xs, attn_norm[j], wqkv)
            q3 = qkv[:, :nq].reshape(dbsz, t_new * nh, hd)
            kn3 = qkv[:, nq:nq + nk].reshape(dbsz, t_new, nk)
            vn3 = qkv[:, nq + nk:].reshape(dbsz, t_new, nk)
            ck = cache_k[j].reshape(dbsz, win, nk)
            cv = cache_v[j].reshape(dbsz, win, nk)
            o3 = _attn_sample(q3, kn3, vn3, ck, cv, attn_sinks[j], nh=nh, nkv=nkv, hd=hd, bb=bb)
            nk_s.append(jnp.concatenate([ck[:, t_new:], kn3], axis=1).reshape(dbsz, win, nkv, hd))
            nv_s.append(jnp.concatenate([cv[:, t_new:], vn3], axis=1).reshape(dbsz, win, nkv, hd))
            xs = _ffn(xs, ffn_norm[layer], wfi, wfo, tm=tm_s, o=o3.reshape(dbsz * t_new, nq), wo=wo, gf=gf)
        else:
            rec_w = (rec_norm[j], w_rec_in[j].astype(BF16), conv_w[j], conv_b[j],
                     w_rgate[j].astype(BF16), b_rgate[j], w_igate[j].astype(BF16), b_igate[j],
                     lru_lambda[j], w_rec_out[j].astype(BF16))
            xp, cp, hp = _rec_prompt(xp, *rec_w, tm=tm)
            nc_p.append(cp)
            nh_p.append(hp.reshape(bsz, d))

            if not time_major:
                xs = xs.reshape(dbsz, t_new, d).transpose(1, 0, 2).reshape(t_new * dbsz, d)
                time_major = True
            sc = state_conv[j].transpose(1, 0, 2)
            xs, cs, hs = _rec_sample(xs, sc, state_h[j], *rec_w, t_new=t_new)
            nc_s.append(cs.transpose(1, 0, 2))
            nh_s.append(hs)
            xs = _ffn(xs, ffn_norm[layer], wfi, wfo, tm=tm_s, gf=gf)
        xp = _ffn(xp.reshape(bsz * seq, d), ffn_norm[layer], wfi, wfo, tm=tm, gf=gf).reshape(bsz, seq, d)

    if time_major:
        ys = xs.reshape(t_new, dbsz, d).transpose(1, 0, 2)
    else:
        ys = xs.reshape(dbsz, t_new, d)
    return (xp, ys, jnp.stack(nk_p), jnp.stack(nv_p), jnp.stack(nk_s), jnp.stack(nv_s),
            jnp.stack(nc_p), jnp.stack(nh_p), jnp.stack(nc_s), jnp.stack(nh_s))
```

```python
import functools

import jax
import jax.numpy as jnp
from jax import lax
from jax.experimental import pallas as pl
from jax.experimental.pallas import tpu as pltpu

EPS = 1e-6
NEG_INF = -1e30
LRU_C = 8.0
CONV_WIDTH = 4
N_LRU_BLOCKS = 4
BLOCK = 128
SUBLANES = 8
LANES = 128
SM_ROWS = 64
TILE_TOKENS = 512
LOG2E = 1.4426950408889634
VMEM_LIMIT = 56 * 1024 * 1024

F32 = jnp.float32
BF16 = jnp.bfloat16


def _rms(x, g):
    return x * lax.rsqrt(jnp.mean(x * x, axis=-1, keepdims=True) + EPS) * g


def _dot(a, b):
    return jnp.dot(a, b, preferred_element_type=F32)


def _dot_nt(a, b):
    return lax.dot_general(a, b, (((1,), (1,)), ((), ())), preferred_element_type=F32)


def _softplus(z):
    return jnp.maximum(z, 0.0) + jnp.log1p(jnp.exp(-jnp.abs(z)))


def _full(shape):
    nd = len(shape)
    return pl.BlockSpec(shape, lambda *_: (0,) * nd)


def _params(sem):
    return pltpu.CompilerParams(dimension_semantics=sem, vmem_limit_bytes=VMEM_LIMIT)


def _attn_prompt_kernel(x_ref, g_ref, wqkv_ref, wo_ref, bias_ref, sink_ref,
                        y_ref, kout_ref, vout_ref,
                        q_scr, klo_scr, khi_scr, vlo_scr, vhi_scr, s_scr, m_scr, p_scr, inv_scr, o_scr,
                        *, tm, nh, nkv, hd):
    i = pl.program_id(1)
    last = pl.num_programs(1) - 1
    grp = nh // nkv
    nq, nk = nh * hd, nkv * hd
    half = (grp // 2) * BLOCK
    unit = 2 * half
    nblk = tm // BLOCK

    x = x_ref[...]
    h = _rms(x, g_ref[...]).astype(BF16)
    qkv = _dot(h, wqkv_ref[...])
    q_scr[...] = (qkv[:, :nq] * (hd ** -0.5 * LOG2E)).astype(BF16)
    k = qkv[:, nq:nq + nk]
    v = qkv[:, nq + nk:]

    @pl.when(i == 0)
    def _():
        for scr in (klo_scr, khi_scr, vlo_scr, vhi_scr):
            scr[0:BLOCK, :] = jnp.zeros((BLOCK, nkv * LANES), BF16)

    @pl.when(i == last)
    def _():
        kout_ref[...] = k[tm - BLOCK:, :]
        vout_ref[...] = v[tm - BLOCK:, :]

    lo = lax.broadcasted_iota(jnp.int32, (1, LANES), 1) < hd
    rows = slice(BLOCK, BLOCK + tm)
    for c in range(nk // LANES):
        for src, lo_scr, hi_scr in ((k, klo_scr, khi_scr), (v, vlo_scr, vhi_scr)):
            chunk = src[:, c * LANES:(c + 1) * LANES]
            swapped = pltpu.roll(chunk, hd, axis=1)
            even = slice(2 * c * LANES, (2 * c + 1) * LANES)
            odd = slice((2 * c + 1) * LANES, (2 * c + 2) * LANES)
            lo_scr[rows, even] = jnp.where(lo, chunk, 0.0).astype(BF16)
            hi_scr[rows, even] = jnp.where(lo, 0.0, swapped).astype(BF16)
            lo_scr[rows, odd] = jnp.where(lo, swapped, 0.0).astype(BF16)
            hi_scr[rows, odd] = jnp.where(lo, 0.0, chunk).astype(BF16)

    def scores(j, buf):
        r0 = j * BLOCK
        for kv in range(nkv):
            qp = jnp.concatenate(
                [q_scr[r0:r0 + BLOCK, (kv * grp + 2 * pr) * hd:(kv * grp + 2 * pr + 2) * hd]
                 for pr in range(grp // 2)], axis=0)
            win = (slice(r0, r0 + 2 * BLOCK), slice(kv * LANES, (kv + 1) * LANES))
            s_scr[buf, kv * unit:kv * unit + half, :] = _dot_nt(qp, klo_scr[win])
            s_scr[buf, kv * unit + half:(kv + 1) * unit, :] = _dot_nt(qp, khi_scr[win])

    def softmax(j, buf):
        sel = jnp.where(i == 0, 1, 0) if j == 0 else 0
        for c in range(nkv * unit // SM_ROWS):
            rs = slice(c * SM_ROWS, (c + 1) * SM_ROWS)
            t = s_scr[buf, rs, :] + bias_ref[sel, rs, :]
            m_scr[rs, :] = jnp.maximum(jnp.max(t, axis=-1, keepdims=True), sink_ref[rs, :])
        for c in range(nkv * unit // SM_ROWS):
            rs = slice(c * SM_ROWS, (c + 1) * SM_ROWS)
            t = s_scr[buf, rs, :] + bias_ref[sel, rs, :]
            m = m_scr[rs, :]
            p = jnp.concatenate([jnp.exp2(t[:, q * LANES:(q + 1) * LANES] - m)
                                 for q in range(2 * BLOCK // LANES)], axis=1)
            p_scr[buf, rs, :] = p.astype(BF16)
            denom = jnp.sum(p, axis=-1, keepdims=True) + jnp.exp2(sink_ref[rs, :] - m)
            inv_scr[rs, :] = 1.0 / denom

    def values(j, buf):
        r0 = j * BLOCK
        for kv in range(nkv):
            win = (slice(r0, r0 + 2 * BLOCK), slice(kv * LANES, (kv + 1) * LANES))
            ev = slice(kv * unit, kv * unit + half)
            od = slice(kv * unit + half, (kv + 1) * unit)
            o = _dot(p_scr[buf, ev, :], vlo_scr[win]) + _dot(p_scr[buf, od, :], vhi_scr[win])
            o = o * jnp.where(lo, inv_scr[ev, :], inv_scr[od, :])
            for pr in range(grp // 2):
                c0 = (kv * grp + 2 * pr) * hd
                o_scr[r0:r0 + BLOCK, c0:c0 + LANES] = o[pr * BLOCK:(pr + 1) * BLOCK, :].astype(BF16)

    scores(0, 0)
    for j in range(nblk):
        if j + 1 < nblk:
            scores(j + 1, (j + 1) % 2)
        softmax(j, j % 2)
        values(j, j % 2)

    y_ref[...] = x + _dot(o_scr[...], wo_ref[...])

    for scr in (klo_scr, khi_scr, vlo_scr, vhi_scr):
        scr[0:BLOCK, :] = scr[tm:tm + BLOCK, :]


def _alibi_slopes(nh):
    return 2.0 ** (-8.0 * jnp.arange(1, nh + 1, dtype=F32) / nh)


def _attn_prompt(x, g, wqkv, wo, sinks, *, tm, nh, nkv, hd):
    bsz, seq, d = x.shape
    grp = nh // nkv
    nk = nkv * hd
    assert 2 * hd == LANES and grp % 2 == 0 and tm % BLOCK == 0
    half = grp // 2 * BLOCK
    r = jnp.arange(grp * BLOCK)
    head_in_kv = 2 * ((r % half) // BLOCK) + r // half
    head = (jnp.arange(nkv)[:, None] * grp + head_in_kv[None, :]).reshape(-1)
    nrows = nkv * grp * BLOCK
    qi = jnp.tile(jnp.arange(BLOCK), nkv * grp)[:, None]
    si = jnp.arange(2 * BLOCK)[None, :]
    dist = qi - si + BLOCK
    band = (dist >= 0) & (dist <= BLOCK)
    slopes = _alibi_slopes(nh)[head][:, None]
    bias = jnp.where(band, -slopes * dist.astype(F32) * LOG2E, NEG_INF)
    bias = jnp.stack([bias, jnp.where(si < BLOCK, NEG_INF, bias)])
    sink = jnp.broadcast_to((sinks.astype(F32) * LOG2E)[head][:, None], (nrows, LANES))

    kern = functools.partial(_attn_prompt_kernel, tm=tm, nh=nh, nkv=nkv, hd=hd)
    return pl.pallas_call(
        kern,
        grid=(bsz, seq // tm),
        in_specs=[
            pl.BlockSpec((None, tm, d), lambda b, i: (b, i, 0)),
            _full((1, d)),
            _full(wqkv.shape),
            _full(wo.shape),
            _full(bias.shape),
            _full(sink.shape),
        ],
        out_specs=[
            pl.BlockSpec((None, tm, d), lambda b, i: (b, i, 0)),
            pl.BlockSpec((None, BLOCK, nk), lambda b, i: (b, 0, 0)),
            pl.BlockSpec((None, BLOCK, nk), lambda b, i: (b, 0, 0)),
        ],
        out_shape=[
            jax.ShapeDtypeStruct((bsz, seq, d), F32),
            jax.ShapeDtypeStruct((bsz, BLOCK, nk), F32),
            jax.ShapeDtypeStruct((bsz, BLOCK, nk), F32),
        ],
        scratch_shapes=[
            pltpu.VMEM((tm, nh * hd), BF16),
            pltpu.VMEM((tm + BLOCK, nkv * LANES), BF16),
            pltpu.VMEM((tm + BLOCK, nkv * LANES), BF16),
            pltpu.VMEM((tm + BLOCK, nkv * LANES), BF16),
            pltpu.VMEM((tm + BLOCK, nkv * LANES), BF16),
            pltpu.VMEM((2, nrows, 2 * BLOCK), F32),
            pltpu.VMEM((nrows, LANES), F32),
            pltpu.VMEM((2, nrows, 2 * BLOCK), BF16),
            pltpu.VMEM((nrows, LANES), F32),
            pltpu.VMEM((tm, nh * hd), BF16),
        ],
        compiler_params=_params(("arbitrary", "arbitrary")),
        name="attn_prompt",
    )(x, g.reshape(1, d), wqkv, wo, bias, sink)


def _rms_matmul_kernel(x_ref, g_ref, w_ref, y_ref):
    h = _rms(x_ref[...], g_ref[...]).astype(BF16)
    y_ref[...] = _dot(h, w_ref[...])


def _rms_matmul(x, g, w):
    n, d = x.shape
    return pl.pallas_call(
        _rms_matmul_kernel,
        grid=(1,),
        in_specs=[_full((n, d)), _full((1, d)), _full(w.shape)],
        out_specs=_full((n, w.shape[1])),
        out_shape=jax.ShapeDtypeStruct((n, w.shape[1]), F32),
        compiler_params=_params(("arbitrary",)),
        name="rms_matmul",
    )(x, g.reshape(1, d), w)


def _attn_sample_kernel(q_ref, kn_ref, vn_ref, ck_ref, cv_ref, e_ref, et_ref, mask_ref,
                        biasc_ref, biasn_ref, sink_ref, o_ref, qx_scr, ox_scr,
                        *, bb, rows, t_new, hd):
    q = (q_ref[...].reshape(bb * rows, hd) * (hd ** -0.5)).astype(BF16)
    mask = mask_ref[...]
    qx_scr[...] = _dot(q, e_ref[...]) * mask
    sink = sink_ref[...]
    biasc = biasc_ref[...]
    biasn = biasn_ref[...]
    for b in range(bb):
        qx = qx_scr[b * rows:(b + 1) * rows, :]
        s_c = _dot_nt(qx.astype(BF16), ck_ref[b].astype(BF16)) + biasc
        kn = kn_ref[b]
        vn = vn_ref[b]
        s_n = [jnp.sum(qx * kn[j:j + 1, :], axis=-1, keepdims=True) + biasn[:, j:j + 1]
               for j in range(t_new)]
        m = jnp.maximum(jnp.max(s_c, axis=-1, keepdims=True), sink)
        for s in s_n:
            m = jnp.maximum(m, s)
        p_c = jnp.exp(s_c - m)
        p_n = [jnp.exp(s - m) for s in s_n]
        denom = jnp.sum(p_c, axis=-1, keepdims=True) + jnp.exp(sink - m)
        for p in p_n:
            denom = denom + p
        o = _dot(p_c.astype(BF16), cv_ref[b].astype(BF16))
        for j in range(t_new):
            o = o + p_n[j] * vn[j:j + 1, :]
        ox_scr[b * rows:(b + 1) * rows, :] = (o * (1.0 / denom) * mask[b * rows:(b + 1) * rows, :]).astype(BF16)
    o_ref[...] = _dot(ox_scr[...], et_ref[...]).reshape(bb, rows, hd)


def _attn_sample(q3, kn3, vn3, ck, cv, sinks, *, nh, nkv, hd, bb):
    dbsz, rows, _ = q3.shape
    t_new = kn3.shape[1]
    win = ck.shape[1]
    nk = nkv * hd
    grp = nh // nkv
    slopes = _alibi_slopes(nh)
    row_t = jnp.arange(rows) // nh
    row_h = jnp.arange(rows) % nh
    row_slope = slopes[row_h][:, None]
    sc = jnp.arange(win)[None, :]
    dist_c = (row_t[:, None] - sc + win).astype(F32)
    biasc = jnp.where(sc >= row_t[:, None], -row_slope * dist_c, NEG_INF)
    sn = jnp.arange(t_new)[None, :]
    dist_n = (row_t[:, None] - sn).astype(F32)
    biasn = jnp.where(sn <= row_t[:, None], -row_slope * dist_n, NEG_INF)
    sink_col = sinks.astype(F32)[row_h][:, None]
    lane_kv = jnp.arange(nk)[None, :] // hd
    mask = (lane_kv == (row_h // grp)[:, None]).astype(F32)
    mask = jnp.tile(mask, (bb, 1))
    e = jnp.tile(jnp.eye(hd, dtype=BF16), (1, nkv))
    et = e.T

    kern = functools.partial(_attn_sample_kernel, bb=bb, rows=rows, t_new=t_new, hd=hd)
    return pl.pallas_call(
        kern,
        grid=(dbsz // bb,),
        in_specs=[
            pl.BlockSpec((bb, rows, hd), lambda i: (i, 0, 0)),
            pl.BlockSpec((bb, t_new, nk), lambda i: (i, 0, 0)),
            pl.BlockSpec((bb, t_new, nk), lambda i: (i, 0, 0)),
            pl.BlockSpec((bb, win, nk), lambda i: (i, 0, 0)),
            pl.BlockSpec((bb, win, nk), lambda i: (i, 0, 0)),
            _full(e.shape), _full(et.shape), _full(mask.shape),
            _full(biasc.shape), _full(biasn.shape), _full(sink_col.shape),
        ],
        out_specs=pl.BlockSpec((bb, rows, hd), lambda i: (i, 0, 0)),
        out_shape=jax.ShapeDtypeStruct((dbsz, rows, hd), F32),
        scratch_shapes=[pltpu.VMEM((bb * rows, nk), F32), pltpu.VMEM((bb * rows, nk), BF16)],
        compiler_params=_params(("arbitrary",)),
        name="attn_sample",
    )(q3, kn3, vn3, ck, cv, e, et, mask, biasc, biasn, sink_col)


def _ffn_kernel(*refs, dff, chunk, proj, final):
    refs = list(refs)
    x_ref = refs.pop(0)
    if proj:
        o_ref, wo_ref = refs.pop(0), refs.pop(0)
    g_ref, win_ref, wout_ref = refs.pop(0), refs.pop(0), refs.pop(0)
    if final:
        gf_ref = refs.pop(0)
    y_ref, a_scr = refs

    x = x_ref[...]
    if proj:
        x = x + _dot(o_ref[...].astype(BF16), wo_ref[...])
    h = _rms(x, g_ref[...]).astype(BF16)
    for c in range(dff // chunk):
        gate = _dot(h, win_ref[:, c * chunk:(c + 1) * chunk])
        up = _dot(h, win_ref[:, dff + c * chunk:dff + (c + 1) * chunk])
        a_scr[:, c * chunk:(c + 1) * chunk] = (gate * jax.nn.sigmoid(gate) * up).astype(BF16)
    y = x + _dot(a_scr[...], wout_ref[...])
    if final:
        y = _rms(y, gf_ref[...])
    y_ref[...] = y


def _ffn(x, g, win, wout, *, tm, o=None, wo=None, gf=None, chunk=256):
    n, d = x.shape
    dff = wout.shape[0]
    proj, final = o is not None, gf is not None
    tile = pl.BlockSpec((tm, d), lambda i: (i, 0))
    args, specs = [x], [tile]
    if proj:
        args += [o, wo]
        specs += [pl.BlockSpec((tm, o.shape[1]), lambda i: (i, 0)), _full(wo.shape)]
    args += [g.reshape(1, d), win, wout]
    specs += [_full((1, d)), _full(win.shape), _full(wout.shape)]
    if final:
        args.append(gf.reshape(1, d))
        specs.append(_full((1, d)))
    kern = functools.partial(_ffn_kernel, dff=dff, chunk=chunk, proj=proj, final=final)
    return pl.pallas_call(
        kern,
        grid=(n // tm,),
        in_specs=specs,
        out_specs=tile,
        out_shape=jax.ShapeDtypeStruct((n, d), F32),
        scratch_shapes=[pltpu.VMEM((tm, dff), BF16)],
        compiler_params=_params(("arbitrary",)),
        name="ffn",
    )(*args)


def _lru_coeffs(xc, wrg_ref, brg_ref, wig_ref, big_ref, lam_ref, a_out, b_out):
    d = xc.shape[1]
    blk = d // N_LRU_BLOCKS
    for n in range(N_LRU_BLOCKS):
        sl = slice(n * blk, (n + 1) * blk)
        xcn = xc[:, sl]
        xcb = xcn.astype(BF16)
        r = jax.nn.sigmoid(_dot(xcb, wrg_ref[n]) + brg_ref[:, sl])
        ig = jax.nn.sigmoid(_dot(xcb, wig_ref[n]) + big_ref[:, sl])
        log_a = -LRU_C * r * _softplus(-lam_ref[:, sl])
        a = jnp.exp(log_a)
        a_out[:, sl] = a
        b_out[:, sl] = jnp.sqrt(-jnp.tanh(log_a) * (a * a + 1.0)) * (ig * xcn)


def _rec_prompt_kernel(x_ref, g_ref, win_ref, cw_ref, cb_ref, wrg_ref, brg_ref, wig_ref, big_ref,
                       lam_ref, wout_ref, y_ref, conv_ref, hout_ref,
                       xb_scr, tail_scr, gate_scr, a_scr, b_scr, hl_scr, pl_scr, carry_scr, *, tm):
    i = pl.program_id(1)
    last = pl.num_programs(1) - 1
    d = x_ref.shape[-1]
    ns = tm // SUBLANES
    npre = CONV_WIDTH - 1
    head = npre * SUBLANES

    @pl.when(i == 0)
    def _():
        tail_scr[...] = jnp.zeros((head, d), F32)
        carry_scr[...] = jnp.zeros((1, d), F32)

    x = x_ref[...].reshape(SUBLANES, ns, d).swapaxes(0, 1).reshape(tm, d)
    h = _rms(x, g_ref[...]).astype(BF16)
    gx = _dot(h, win_ref[...])
    gate_scr[...] = jax.nn.gelu(gx[:, :d])
    xb_scr[head:head + tm, :] = gx[:, d:]

    seg = lax.broadcasted_iota(jnp.int32, (SUBLANES, 1), 0)
    for k in range(npre):
        rows = slice(k * SUBLANES, (k + 1) * SUBLANES)
        own = pltpu.roll(xb_scr[tm + k * SUBLANES:tm + (k + 1) * SUBLANES, :], 1, axis=0)
        prev = pltpu.roll(tail_scr[rows, :], 1, axis=0)
        xb_scr[rows, :] = jnp.where(seg == 0, prev, own)
    tail_scr[...] = xb_scr[tm:tm + head, :]

    xc = cb_ref[...]
    for j in range(CONV_WIDTH):
        xc = xc + xb_scr[j * SUBLANES:j * SUBLANES + tm, :] * cw_ref[j:j + 1, :]

    _lru_coeffs(xc, wrg_ref, brg_ref, wig_ref, big_ref, lam_ref, a_scr, b_scr)

    def body(s, carry):
        hloc, prod = carry
        rows = pl.ds(pl.multiple_of(s * SUBLANES, SUBLANES), SUBLANES)
        a = a_scr[rows, :]
        hloc = a * hloc + b_scr[rows, :]
        prod = a * prod
        hl_scr[rows, :] = hloc
        pl_scr[rows, :] = prod
        return hloc, prod

    hend, pend = lax.fori_loop(0, ns, body, (jnp.zeros((SUBLANES, d), F32), jnp.ones((SUBLANES, d), F32)),
                               unroll=4)

    a = jnp.where(seg == 0, 0.0, pltpu.roll(pend, 1, axis=0))
    b = jnp.where(seg == 0, carry_scr[...], pltpu.roll(hend, 1, axis=0))
    shift = 1
    while shift < SUBLANES:
        keep = seg >= shift
        b = jnp.where(keep, a * pltpu.roll(b, shift, axis=0) + b, b)
        a = jnp.where(keep, a * pltpu.roll(a, shift, axis=0), a)
        shift *= 2
    hstart = b
    hfinal = hend + pend * hstart
    carry = hfinal[SUBLANES - 1:SUBLANES, :]
    carry_scr[...] = carry

    hs = hl_scr[...].reshape(ns, SUBLANES, d) + pl_scr[...].reshape(ns, SUBLANES, d) * hstart[None]
    y = _dot((hs.reshape(tm, d) * gate_scr[...]).astype(BF16), wout_ref[...])
    y_ref[...] = x_ref[...] + y.reshape(ns, SUBLANES, d).swapaxes(0, 1).reshape(tm, d)

    @pl.when(i == last)
    def _():
        for k in range(npre):
            r = head + (ns - npre + k) * SUBLANES + SUBLANES - 1
            conv_ref[k:k + 1, :] = xb_scr[r:r + 1, :]
        hout_ref[...] = carry


def _rec_weight_specs(d, win, wrg, wig, wout):
    return [_full((1, d)), _full(win.shape), _full((CONV_WIDTH, d)), _full((1, d)),
            _full(wrg.shape), _full((1, d)), _full(wig.shape), _full((1, d)), _full((1, d)),
            _full(wout.shape)]


def _rec_prompt(x, g, win, cw, cb, wrg, brg, wig, big, lam, wout, *, tm):
    bsz, seq, d = x.shape
    kern = functools.partial(_rec_prompt_kernel, tm=tm)
    head = (CONV_WIDTH - 1) * SUBLANES
    assert tm // SUBLANES >= CONV_WIDTH - 1
    return pl.pallas_call(
        kern,
        grid=(bsz, seq // tm),
        in_specs=[pl.BlockSpec((None, tm, d), lambda b, i: (b, i, 0))]
        + _rec_weight_specs(d, win, wrg, wig, wout),
        out_specs=[
            pl.BlockSpec((None, tm, d), lambda b, i: (b, i, 0)),
            pl.BlockSpec((None, CONV_WIDTH - 1, d), lambda b, i: (b, 0, 0)),
            pl.BlockSpec((None, 1, d), lambda b, i: (b, 0, 0)),
        ],
        out_shape=[
            jax.ShapeDtypeStruct((bsz, seq, d), F32),
            jax.ShapeDtypeStruct((bsz, CONV_WIDTH - 1, d), F32),
            jax.ShapeDtypeStruct((bsz, 1, d), F32),
        ],
        scratch_shapes=[
            pltpu.VMEM((tm + head, d), F32),
            pltpu.VMEM((head, d), F32),
            pltpu.VMEM((tm, d), F32),
            pltpu.VMEM((tm, d), F32),
            pltpu.VMEM((tm, d), F32),
            pltpu.VMEM((tm, d), F32),
            pltpu.VMEM((tm, d), F32),
            pltpu.VMEM((1, d), F32),
        ],
        compiler_params=_params(("arbitrary", "arbitrary")),
        name="rec_prompt",
    )(x, g.reshape(1, d), win, cw, cb.reshape(1, d), wrg, brg.reshape(1, d), wig,
      big.reshape(1, d), lam.reshape(1, d), wout)


def _rec_sample_kernel(x_ref, sc_ref, h0_ref, g_ref, win_ref, cw_ref, cb_ref, wrg_ref, brg_ref,
                       wig_ref, big_ref, lam_ref, wout_ref, y_ref, conv_ref, hout_ref,
                       a_scr, b_scr, hs_scr, *, t_new, nb):
    d = x_ref.shape[-1]
    x = x_ref[...]
    h = _rms(x, g_ref[...]).astype(BF16)
    gx = _dot(h, win_ref[...])
    gate = gx[:, :d]
    xb = gx[:, d:]
    xp = [sc_ref[j] for j in range(CONV_WIDTH - 1)] + [xb[t * nb:(t + 1) * nb, :] for t in range(t_new)]
    xcs = []
    for t in range(t_new):
        acc = cb_ref[...]
        for j in range(CONV_WIDTH):
            acc = acc + xp[t + j] * cw_ref[j:j + 1, :]
        xcs.append(acc)
    xc = jnp.concatenate(xcs, axis=0)

    _lru_coeffs(xc, wrg_ref, brg_ref, wig_ref, big_ref, lam_ref, a_scr, b_scr)

    hprev = h0_ref[...]
    for t in range(t_new):
        sl = slice(t * nb, (t + 1) * nb)
        hprev = a_scr[sl, :] * hprev + b_scr[sl, :]
        hs_scr[sl, :] = hprev
    y_ref[...] = x + _dot((hs_scr[...] * jax.nn.gelu(gate)).astype(BF16), wout_ref[...])
    for j in range(CONV_WIDTH - 1):
        conv_ref[j] = xp[t_new + j]
    hout_ref[...] = hprev


def _rec_sample(x, sc, h0, g, win, cw, cb, wrg, brg, wig, big, lam, wout, *, t_new):
    n, d = x.shape
    nb = n // t_new
    kern = functools.partial(_rec_sample_kernel, t_new=t_new, nb=nb)
    return pl.pallas_call(
        kern,
        grid=(1,),
        in_specs=[_full((n, d)), _full(sc.shape), _full(h0.shape)]
        + _rec_weight_specs(d, win, wrg, wig, wout),
        out_specs=[_full((n, d)), _full(sc.shape), _full(h0.shape)],
        out_shape=[
            jax.ShapeDtypeStruct((n, d), F32),
            jax.ShapeDtypeStruct(sc.shape, F32),
            jax.ShapeDtypeStruct(h0.shape, F32),
        ],
        scratch_shapes=[pltpu.VMEM((n, d), F32)] * 3,
        compiler_params=_params(("arbitrary",)),
        name="rec_sample",
    )(x, sc, h0, g.reshape(1, d), win, cw, cb.reshape(1, d), wrg, brg.reshape(1, d), wig,
      big.reshape(1, d), lam.reshape(1, d), wout)


def _pick_tile(seq, want):
    tm = min(want, seq)
    while seq % tm:
        tm //= 2
    return tm


def kernel(x_prompt, x_sample, cache_k, cache_v, state_conv, state_h, attn_norm, w_qkv, w_attn_out,
           attn_sinks, rec_norm, w_rec_in, conv_w, conv_b, w_rgate, b_rgate, w_igate, b_igate,
           lru_lambda, w_rec_out, ffn_norm, w_ffn_in, w_ffn_out, final_norm):
    bsz, seq, d = x_prompt.shape
    dbsz, t_new, _ = x_sample.shape
    depth = ffn_norm.shape[0]
    nh = attn_sinks.shape[1]
    hd = w_attn_out.shape[1] // nh
    nkv = (w_qkv.shape[2] // hd - nh) // 2
    nq, nk = nh * hd, nkv * hd
    win = cache_k.shape[2]
    tm = _pick_tile(seq, TILE_TOKENS)
    tm_s = dbsz * t_new
    bb = _pick_tile(dbsz, 8)

    xp = x_prompt
    xs = x_sample.reshape(dbsz * t_new, d)
    time_major = False
    nk_p, nv_p, nk_s, nv_s, nc_p, nh_p, nc_s, nh_s = [], [], [], [], [], [], [], []

    for layer in range(depth):
        j = layer // 2
        wfi = w_ffn_in[layer].astype(BF16)
        wfo = w_ffn_out[layer].astype(BF16)
        gf = final_norm if layer == depth - 1 else None
        if layer % 2 == 0:
            wqkv = w_qkv[j].astype(BF16)
            wo = w_attn_out[j].astype(BF16)
            xp, kp, vp = _attn_prompt(xp, attn_norm[j], wqkv, wo, attn_sinks[j],
                                      tm=tm, nh=nh, nkv=nkv, hd=hd)
            nk_p.append(kp.reshape(bsz, BLOCK, nkv, hd))
            nv_p.append(vp.reshape(bsz, BLOCK, nkv, hd))

            if time_major:
                xs = xs.reshape(t_new, dbsz, d).transpose(1, 0, 2).reshape(dbsz * t_new, d)
                time_major = False
            qkv = _rms_matmul(xs, attn_norm[j], wqkv)
            q3 = qkv[:, :nq].reshape(dbsz, t_new * nh, hd)
            kn3 = qkv[:, nq:nq + nk].reshape(dbsz, t_new, nk)
            vn3 = qkv[:, nq + nk:].reshape(dbsz, t_new, nk)
            ck = cache_k[j].reshape(dbsz, win, nk)
            cv = cache_v[j].reshape(dbsz, win, nk)
            o3 = _attn_sample(q3, kn3, vn3, ck, cv, attn_sinks[j], nh=nh, nkv=nkv, hd=hd, bb=bb)
            nk_s.append(jnp.concatenate([ck[:, t_new:], kn3], axis=1).reshape(dbsz, win, nkv, hd))
            nv_s.append(jnp.concatenate([cv[:, t_new:], vn3], axis=1).reshape(dbsz, win, nkv, hd))
            xs = _ffn(xs, ffn_norm[layer], wfi, wfo, tm=tm_s, o=o3.reshape(dbsz * t_new, nq), wo=wo, gf=gf)
        else:
            rec_w = (rec_norm[j], w_rec_in[j].astype(BF16), conv_w[j], conv_b[j],
                     w_rgate[j].astype(BF16), b_rgate[j], w_igate[j].astype(BF16), b_igate[j],
                     lru_lambda[j], w_rec_out[j].astype(BF16))
            xp, cp, hp = _rec_prompt(xp, *rec_w, tm=tm)
            nc_p.append(cp)
            nh_p.append(hp.reshape(bsz, d))

            if not time_major:
                xs = xs.reshape(dbsz, t_new, d).transpose(1, 0, 2).reshape(t_new * dbsz, d)
                time_major = True
            sc = state_conv[j].transpose(1, 0, 2)
            xs, cs, hs = _rec_sample(xs, sc, state_h[j], *rec_w, t_new=t_new)
            nc_s.append(cs.transpose(1, 0, 2))
            nh_s.append(hs)
            xs = _ffn(xs, ffn_norm[layer], wfi, wfo, tm=tm_s, gf=gf)
        xp = _ffn(xp.reshape(bsz * seq, d), ffn_norm[layer], wfi, wfo, tm=tm, gf=gf).reshape(bsz, seq, d)

    if time_major:
        ys = xs.reshape(t_new, dbsz, d).transpose(1, 0, 2)
    else:
        ys = xs.reshape(dbsz, t_new, d)
    return (xp, ys, jnp.stack(nk_p), jnp.stack(nv_p), jnp.stack(nk_s), jnp.stack(nv_s),
            jnp.stack(nc_p), jnp.stack(nh_p), jnp.stack(nc_s), jnp.stack(nh_s))
```

```python
import functools

import numpy as np

import jax
import jax.numpy as jnp
from jax import lax
from jax.experimental import pallas as pl
from jax.experimental.pallas import tpu as pltpu

EPS = 1e-6
NEG_INF = -1e30
LRU_C = 8.0
CONV_WIDTH = 4
N_LRU_BLOCKS = 4
BLOCK = 128
SUBLANES = 8
LANES = 128
SM_ROWS = 64
TILE_TOKENS = 512
LOG2E = 1.4426950408889634
VMEM_LIMIT = 56 * 1024 * 1024

F32 = jnp.float32
BF16 = jnp.bfloat16


def _rms(x, g):
    return x * lax.rsqrt(jnp.mean(x * x, axis=-1, keepdims=True) + EPS) * g


def _dot(a, b):
    return jnp.dot(a, b, preferred_element_type=F32)


def _dot_nt(a, b):
    return lax.dot_general(a, b, (((1,), (1,)), ((), ())), preferred_element_type=F32)


def _softplus(z):
    return jnp.maximum(z, 0.0) + jnp.log1p(jnp.exp(-jnp.abs(z)))


def _full(shape):
    nd = len(shape)
    return pl.BlockSpec(shape, lambda *_: (0,) * nd)


def _params(sem):
    return pltpu.CompilerParams(dimension_semantics=sem, vmem_limit_bytes=VMEM_LIMIT)


def _attn_prompt_kernel(x_ref, g_ref, wqkv_ref, wo_ref, bias_ref, sink_ref,
                        y_ref, kout_ref, vout_ref,
                        q_scr, klo_scr, khi_scr, vlo_scr, vhi_scr, s_scr, m_scr, p_scr, inv_scr, o_scr,
                        *, tm, nh, nkv, hd):
    i = pl.program_id(1)
    last = pl.num_programs(1) - 1
    grp = nh // nkv
    nq, nk = nh * hd, nkv * hd
    half = (grp // 2) * BLOCK
    unit = 2 * half
    nblk = tm // BLOCK

    x = x_ref[...]
    h = _rms(x, g_ref[...]).astype(BF16)
    qkv = _dot(h, wqkv_ref[...])
    q_scr[...] = (qkv[:, :nq] * (hd ** -0.5 * LOG2E)).astype(BF16)
    k = qkv[:, nq:nq + nk]
    v = qkv[:, nq + nk:]

    @pl.when(i == 0)
    def _():
        for scr in (klo_scr, khi_scr, vlo_scr, vhi_scr):
            scr[0:BLOCK, :] = jnp.zeros((BLOCK, nkv * LANES), BF16)

    @pl.when(i == last)
    def _():
        kout_ref[...] = k[tm - BLOCK:, :]
        vout_ref[...] = v[tm - BLOCK:, :]

    lo = lax.broadcasted_iota(jnp.int32, (1, LANES), 1) < hd
    rows = slice(BLOCK, BLOCK + tm)
    for c in range(nk // LANES):
        for src, lo_scr, hi_scr in ((k, klo_scr, khi_scr), (v, vlo_scr, vhi_scr)):
            chunk = src[:, c * LANES:(c + 1) * LANES]
            swapped = pltpu.roll(chunk, hd, axis=1)
            even = slice(2 * c * LANES, (2 * c + 1) * LANES)
            odd = slice((2 * c + 1) * LANES, (2 * c + 2) * LANES)
            lo_scr[rows, even] = jnp.where(lo, chunk, 0.0).astype(BF16)
            hi_scr[rows, even] = jnp.where(lo, 0.0, swapped).astype(BF16)
            lo_scr[rows, odd] = jnp.where(lo, swapped, 0.0).astype(BF16)
            hi_scr[rows, odd] = jnp.where(lo, 0.0, chunk).astype(BF16)

    def scores(j, buf):
        r0 = j * BLOCK
        for kv in range(nkv):
            qp = jnp.concatenate(
                [q_scr[r0:r0 + BLOCK, (kv * grp + 2 * pr) * hd:(kv * grp + 2 * pr + 2) * hd]
                 for pr in range(grp // 2)], axis=0)
            win = (slice(r0, r0 + 2 * BLOCK), slice(kv * LANES, (kv + 1) * LANES))
            s_scr[buf, kv * unit:kv * unit + half, :] = _dot_nt(qp, klo_scr[win])
            s_scr[buf, kv * unit + half:(kv + 1) * unit, :] = _dot_nt(qp, khi_scr[win])

    def softmax(j, buf):
        sel = jnp.where(i == 0, 1, 0) if j == 0 else 0
        for c in range(nkv * unit // SM_ROWS):
            rs = slice(c * SM_ROWS, (c + 1) * SM_ROWS)
            t = s_scr[buf, rs, :] + bias_ref[sel, rs, :]
            m_scr[rs, :] = jnp.maximum(jnp.max(t, axis=-1, keepdims=True), sink_ref[rs, :])
        for c in range(nkv * unit // SM_ROWS):
            rs = slice(c * SM_ROWS, (c + 1) * SM_ROWS)
            t = s_scr[buf, rs, :] + bias_ref[sel, rs, :]
            m = m_scr[rs, :]
            p = jnp.concatenate([jnp.exp2(t[:, q * LANES:(q + 1) * LANES] - m)
                                 for q in range(2 * BLOCK // LANES)], axis=1)
            p_scr[buf, rs, :] = p.astype(BF16)
            denom = jnp.sum(p, axis=-1, keepdims=True) + jnp.exp2(sink_ref[rs, :] - m)
            inv_scr[rs, :] = 1.0 / denom

    def values(j, buf):
        r0 = j * BLOCK
        for kv in range(nkv):
            win = (slice(r0, r0 + 2 * BLOCK), slice(kv * LANES, (kv + 1) * LANES))
            ev = slice(kv * unit, kv * unit + half)
            od = slice(kv * unit + half, (kv + 1) * unit)
            o = _dot(p_scr[buf, ev, :], vlo_scr[win]) + _dot(p_scr[buf, od, :], vhi_scr[win])
            o = o * jnp.where(lo, inv_scr[ev, :], inv_scr[od, :])
            for pr in range(grp // 2):
                c0 = (kv * grp + 2 * pr) * hd
                o_scr[r0:r0 + BLOCK, c0:c0 + LANES] = o[pr * BLOCK:(pr + 1) * BLOCK, :].astype(BF16)

    scores(0, 0)
    for j in range(nblk):
        if j + 1 < nblk:
            scores(j + 1, (j + 1) % 2)
        softmax(j, j % 2)
        values(j, j % 2)

    y_ref[...] = x + _dot(o_scr[...], wo_ref[...])

    for scr in (klo_scr, khi_scr, vlo_scr, vhi_scr):
        scr[0:BLOCK, :] = scr[tm:tm + BLOCK, :]


def _alibi_slopes(nh):
    return (2.0 ** (-8.0 * np.arange(1, nh + 1, dtype=np.float32) / nh)).astype(np.float32)


def _attn_prompt(x, g, wqkv, wo, sinks, *, tm, nh, nkv, hd):
    bsz, seq, d = x.shape
    grp = nh // nkv
    nk = nkv * hd
    assert 2 * hd == LANES and grp % 2 == 0 and tm % BLOCK == 0
    half = grp // 2 * BLOCK
    r = np.arange(grp * BLOCK)
    head_in_kv = 2 * ((r % half) // BLOCK) + r // half
    head = (np.arange(nkv)[:, None] * grp + head_in_kv[None, :]).reshape(-1)
    nrows = nkv * grp * BLOCK
    qi = np.tile(np.arange(BLOCK), nkv * grp)[:, None]
    si = np.arange(2 * BLOCK)[None, :]
    dist = qi - si + BLOCK
    band = (dist >= 0) & (dist <= BLOCK)
    slopes = _alibi_slopes(nh)[head][:, None]
    bias = np.where(band, -slopes * dist.astype(np.float32) * np.float32(LOG2E), np.float32(NEG_INF))
    bias = np.stack([bias, np.where(si < BLOCK, np.float32(NEG_INF), bias)]).astype(np.float32)
    sink = jnp.broadcast_to((sinks.astype(F32) * LOG2E)[head][:, None], (nrows, LANES))

    kern = functools.partial(_attn_prompt_kernel, tm=tm, nh=nh, nkv=nkv, hd=hd)
    return pl.pallas_call(
        kern,
        grid=(bsz, seq // tm),
        in_specs=[
            pl.BlockSpec((None, tm, d), lambda b, i: (b, i, 0)),
            _full((1, d)),
            _full(wqkv.shape),
            _full(wo.shape),
            _full(bias.shape),
            _full(sink.shape),
        ],
        out_specs=[
            pl.BlockSpec((None, tm, d), lambda b, i: (b, i, 0)),
            pl.BlockSpec((None, BLOCK, nk), lambda b, i: (b, 0, 0)),
            pl.BlockSpec((None, BLOCK, nk), lambda b, i: (b, 0, 0)),
        ],
        out_shape=[
            jax.ShapeDtypeStruct((bsz, seq, d), F32),
            jax.ShapeDtypeStruct((bsz, BLOCK, nk), F32),
            jax.ShapeDtypeStruct((bsz, BLOCK, nk), F32),
        ],
        scratch_shapes=[
            pltpu.VMEM((tm, nh * hd), BF16),
            pltpu.VMEM((tm + BLOCK, nkv * LANES), BF16),
            pltpu.VMEM((tm + BLOCK, nkv * LANES), BF16),
            pltpu.VMEM((tm + BLOCK, nkv * LANES), BF16),
            pltpu.VMEM((tm + BLOCK, nkv * LANES), BF16),
            pltpu.VMEM((2, nrows, 2 * BLOCK), F32),
            pltpu.VMEM((nrows, LANES), F32),
            pltpu.VMEM((2, nrows, 2 * BLOCK), BF16),
            pltpu.VMEM((nrows, LANES), F32),
            pltpu.VMEM((tm, nh * hd), BF16),
        ],
        compiler_params=_params(("arbitrary", "arbitrary")),
        name="attn_prompt",
    )(x, g.reshape(1, d), wqkv, wo, bias, sink)


def _rms_matmul_kernel(x_ref, g_ref, w_ref, y_ref):
    h = _rms(x_ref[...], g_ref[...]).astype(BF16)
    y_ref[...] = _dot(h, w_ref[...])


def _rms_matmul(x, g, w):
    n, d = x.shape
    return pl.pallas_call(
        _rms_matmul_kernel,
        grid=(1,),
        in_specs=[_full((n, d)), _full((1, d)), _full(w.shape)],
        out_specs=_full((n, w.shape[1])),
        out_shape=jax.ShapeDtypeStruct((n, w.shape[1]), F32),
        compiler_params=_params(("arbitrary",)),
        name="rms_matmul",
    )(x, g.reshape(1, d), w)


def _attn_sample_kernel(q_ref, kn_ref, vn_ref, ck_ref, cv_ref, e_ref, et_ref, mask_ref,
                        bias_ref, sink_ref, o_ref, qx_scr, kk_scr, vv_scr, s_scr, p_scr, ox_scr,
                        *, bb, rows, t_new, hd):
    win = ck_ref.shape[1]

    @pl.when(pl.program_id(0) == 0)
    def _():
        kk_scr[:, win:, :] = jnp.zeros((bb, win, kk_scr.shape[2]), BF16)
        vv_scr[:, win:, :] = jnp.zeros((bb, win, vv_scr.shape[2]), BF16)

    q = (q_ref[...].reshape(bb * rows, hd) * (hd ** -0.5 * LOG2E)).astype(BF16)
    mask = mask_ref[...]
    qx_scr[...] = (_dot(q, e_ref[...]) * mask).astype(BF16)
    for b in range(bb):
        kk_scr[b, 0:win, :] = ck_ref[b].astype(BF16)
        kk_scr[b, win:win + t_new, :] = kn_ref[b].astype(BF16)
        vv_scr[b, 0:win, :] = cv_ref[b].astype(BF16)
        vv_scr[b, win:win + t_new, :] = vn_ref[b].astype(BF16)
    for b in range(bb):
        rs = slice(b * rows, (b + 1) * rows)
        s_scr[rs, :] = _dot_nt(qx_scr[rs, :], kk_scr[b])

    t = s_scr[...] + bias_ref[...]
    sink = sink_ref[...]
    m = jnp.maximum(jnp.max(t, axis=-1, keepdims=True), sink)
    p = jnp.concatenate([jnp.exp2(t[:, c * LANES:(c + 1) * LANES] - m)
                         for c in range(2 * win // LANES)], axis=1)
    p_scr[...] = p.astype(BF16)
    inv = 1.0 / (jnp.sum(p, axis=-1, keepdims=True) + jnp.exp2(sink - m))
    scale = jnp.concatenate([inv] * (mask.shape[1] // LANES), axis=1) * mask
    for b in range(bb):
        rs = slice(b * rows, (b + 1) * rows)
        ox_scr[rs, :] = (_dot(p_scr[rs, :], vv_scr[b]) * scale[rs, :]).astype(BF16)
    o_ref[...] = _dot(ox_scr[...], et_ref[...]).reshape(bb, rows, hd)


def _attn_sample(q3, kn3, vn3, ck, cv, sinks, *, nh, nkv, hd, bb):
    dbsz, rows, _ = q3.shape
    t_new = kn3.shape[1]
    win = ck.shape[1]
    nk = nkv * hd
    grp = nh // nkv
    assert t_new <= win and win % LANES == 0 and nk % LANES == 0
    row_t = np.arange(rows)[:, None] // nh
    row_h = np.arange(rows) % nh
    row_slope = _alibi_slopes(nh)[row_h][:, None]
    key = np.arange(2 * win)[None, :]
    dist = (row_t - key + win).astype(np.float32)
    valid = (key >= row_t) & (key <= row_t + win) & (key < win + t_new)
    bias = np.where(valid, -row_slope * dist * np.float32(LOG2E), np.float32(NEG_INF)).astype(np.float32)
    bias = np.tile(bias, (bb, 1))
    mask = (np.arange(nk)[None, :] // hd == (row_h // grp)[:, None]).astype(np.float32)
    mask = np.tile(mask, (bb, 1))
    e = np.tile(np.eye(hd, dtype=np.float32), (1, nkv))
    sink = jnp.broadcast_to((sinks.astype(F32) * LOG2E)[np.tile(row_h, bb)][:, None], (bb * rows, LANES))

    kern = functools.partial(_attn_sample_kernel, bb=bb, rows=rows, t_new=t_new, hd=hd)
    return pl.pallas_call(
        kern,
        grid=(dbsz // bb,),
        in_specs=[
            pl.BlockSpec((bb, rows, hd), lambda i: (i, 0, 0)),
            pl.BlockSpec((bb, t_new, nk), lambda i: (i, 0, 0)),
            pl.BlockSpec((bb, t_new, nk), lambda i: (i, 0, 0)),
            pl.BlockSpec((bb, win, nk), lambda i: (i, 0, 0)),
            pl.BlockSpec((bb, win, nk), lambda i: (i, 0, 0)),
            _full(e.shape), _full(e.T.shape), _full(mask.shape),
            _full(bias.shape), _full(sink.shape),
        ],
        out_specs=pl.BlockSpec((bb, rows, hd), lambda i: (i, 0, 0)),
        out_shape=jax.ShapeDtypeStruct((dbsz, rows, hd), F32),
        scratch_shapes=[
            pltpu.VMEM((bb * rows, nk), BF16),
            pltpu.VMEM((bb, 2 * win, nk), BF16),
            pltpu.VMEM((bb, 2 * win, nk), BF16),
            pltpu.VMEM((bb * rows, 2 * win), F32),
            pltpu.VMEM((bb * rows, 2 * win), BF16),
            pltpu.VMEM((bb * rows, nk), BF16),
        ],
        compiler_params=_params(("arbitrary",)),
        name="attn_sample",
    )(q3, kn3, vn3, ck, cv, jnp.asarray(e, BF16), jnp.asarray(e.T, BF16), mask, bias, sink)


def _ffn_kernel(*refs, dff, chunk, proj, final):
    refs = list(refs)
    x_ref = refs.pop(0)
    if proj:
        o_ref, wo_ref = refs.pop(0), refs.pop(0)
    g_ref, win_ref, wout_ref = refs.pop(0), refs.pop(0), refs.pop(0)
    if final:
        gf_ref = refs.pop(0)
    y_ref, a_scr = refs

    x = x_ref[...]
    if proj:
        x = x + _dot(o_ref[...].astype(BF16), wo_ref[...])
    h = _rms(x, g_ref[...]).astype(BF16)
    for c in range(dff // chunk):
        gate = _dot(h, win_ref[:, c * chunk:(c + 1) * chunk])
        up = _dot(h, win_ref[:, dff + c * chunk:dff + (c + 1) * chunk])
        a_scr[:, c * chunk:(c + 1) * chunk] = (gate * jax.nn.sigmoid(gate) * up).astype(BF16)
    y = x + _dot(a_scr[...], wout_ref[...])
    if final:
        y = _rms(y, gf_ref[...])
    y_ref[...] = y


def _ffn(x, g, win, wout, *, tm, o=None, wo=None, gf=None, chunk=256):
    n, d = x.shape
    dff = wout.shape[0]
    proj, final = o is not None, gf is not None
    tile = pl.BlockSpec((tm, d), lambda i: (i, 0))
    args, specs = [x], [tile]
    if proj:
        args += [o, wo]
        specs += [pl.BlockSpec((tm, o.shape[1]), lambda i: (i, 0)), _full(wo.shape)]
    args += [g.reshape(1, d), win, wout]
    specs += [_full((1, d)), _full(win.shape), _full(wout.shape)]
    if final:
        args.append(gf.reshape(1, d))
        specs.append(_full((1, d)))
    kern = functools.partial(_ffn_kernel, dff=dff, chunk=chunk, proj=proj, final=final)
    return pl.pallas_call(
        kern,
        grid=(n // tm,),
        in_specs=specs,
        out_specs=tile,
        out_shape=jax.ShapeDtypeStruct((n, d), F32),
        scratch_shapes=[pltpu.VMEM((tm, dff), BF16)],
        compiler_params=_params(("arbitrary",)),
        name="ffn",
    )(*args)


def _lru_coeffs(xc, wrg_ref, brg_ref, wig_ref, big_ref, lam_ref, a_out, b_out):
    d = xc.shape[1]
    blk = d // N_LRU_BLOCKS
    for n in range(N_LRU_BLOCKS):
        sl = slice(n * blk, (n + 1) * blk)
        xcn = xc[:, sl]
        xcb = xcn.astype(BF16)
        r = jax.nn.sigmoid(_dot(xcb, wrg_ref[n]) + brg_ref[:, sl])
        ig = jax.nn.sigmoid(_dot(xcb, wig_ref[n]) + big_ref[:, sl])
        log_a = -LRU_C * r * _softplus(-lam_ref[:, sl])
        a = jnp.exp(log_a)
        a_out[:, sl] = a
        b_out[:, sl] = jnp.sqrt(-jnp.tanh(log_a) * (a * a + 1.0)) * (ig * xcn)


def _rec_prompt_kernel(x_ref, g_ref, win_ref, cw_ref, cb_ref, wrg_ref, brg_ref, wig_ref, big_ref,
                       lam_ref, wout_ref, y_ref, conv_ref, hout_ref,
                       xb_scr, tail_scr, gate_scr, a_scr, b_scr, hl_scr, pl_scr, carry_scr, *, tm):
    i = pl.program_id(1)
    last = pl.num_programs(1) - 1
    d = x_ref.shape[-1]
    ns = tm // SUBLANES
    npre = CONV_WIDTH - 1
    head = npre * SUBLANES

    @pl.when(i == 0)
    def _():
        tail_scr[...] = jnp.zeros((head, d), F32)
        carry_scr[...] = jnp.zeros((1, d), F32)

    x = x_ref[...].reshape(SUBLANES, ns, d).swapaxes(0, 1).reshape(tm, d)
    h = _rms(x, g_ref[...]).astype(BF16)
    gx = _dot(h, win_ref[...])
    gate_scr[...] = jax.nn.gelu(gx[:, :d])
    xb_scr[head:head + tm, :] = gx[:, d:]

    seg = lax.broadcasted_iota(jnp.int32, (SUBLANES, 1), 0)
    for k in range(npre):
        rows = slice(k * SUBLANES, (k + 1) * SUBLANES)
        own = pltpu.roll(xb_scr[tm + k * SUBLANES:tm + (k + 1) * SUBLANES, :], 1, axis=0)
        prev = pltpu.roll(tail_scr[rows, :], 1, axis=0)
        xb_scr[rows, :] = jnp.where(seg == 0, prev, own)
    tail_scr[...] = xb_scr[tm:tm + head, :]

    xc = cb_ref[...]
    for j in range(CONV_WIDTH):
        xc = xc + xb_scr[j * SUBLANES:j * SUBLANES + tm, :] * cw_ref[j:j + 1, :]

    _lru_coeffs(xc, wrg_ref, brg_ref, wig_ref, big_ref, lam_ref, a_scr, b_scr)

    def body(s, carry):
        hloc, prod = carry
        rows = pl.ds(pl.multiple_of(s * SUBLANES, SUBLANES), SUBLANES)
        a = a_scr[rows, :]
        hloc = a * hloc + b_scr[rows, :]
        prod = a * prod
        hl_scr[rows, :] = hloc
        pl_scr[rows, :] = prod
        return hloc, prod

    hend, pend = lax.fori_loop(0, ns, body, (jnp.zeros((SUBLANES, d), F32), jnp.ones((SUBLANES, d), F32)),
                               unroll=4)

    a = jnp.where(seg == 0, 0.0, pltpu.roll(pend, 1, axis=0))
    b = jnp.where(seg == 0, carry_scr[...], pltpu.roll(hend, 1, axis=0))
    shift = 1
    while shift < SUBLANES:
        keep = seg >= shift
        b = jnp.where(keep, a * pltpu.roll(b, shift, axis=0) + b, b)
        a = jnp.where(keep, a * pltpu.roll(a, shift, axis=0), a)
        shift *= 2
    hstart = b
    hfinal = hend + pend * hstart
    carry = hfinal[SUBLANES - 1:SUBLANES, :]
    carry_scr[...] = carry

    hs = hl_scr[...].reshape(ns, SUBLANES, d) + pl_scr[...].reshape(ns, SUBLANES, d) * hstart[None]
    y = _dot((hs.reshape(tm, d) * gate_scr[...]).astype(BF16), wout_ref[...])
    y_ref[...] = x_ref[...] + y.reshape(ns, SUBLANES, d).swapaxes(0, 1).reshape(tm, d)

    @pl.when(i == last)
    def _():
        for k in range(npre):
            r = head + (ns - npre + k) * SUBLANES + SUBLANES - 1
            conv_ref[k:k + 1, :] = xb_scr[r:r + 1, :]
        hout_ref[...] = carry


def _rec_weight_specs(d, win, wrg, wig, wout):
    return [_full((1, d)), _full(win.shape), _full((CONV_WIDTH, d)), _full((1, d)),
            _full(wrg.shape), _full((1, d)), _full(wig.shape), _full((1, d)), _full((1, d)),
            _full(wout.shape)]


def _rec_prompt(x, g, win, cw, cb, wrg, brg, wig, big, lam, wout, *, tm):
    bsz, seq, d = x.shape
    kern = functools.partial(_rec_prompt_kernel, tm=tm)
    head = (CONV_WIDTH - 1) * SUBLANES
    assert tm // SUBLANES >= CONV_WIDTH - 1
    return pl.pallas_call(
        kern,
        grid=(bsz, seq // tm),
        in_specs=[pl.BlockSpec((None, tm, d), lambda b, i: (b, i, 0))]
        + _rec_weight_specs(d, win, wrg, wig, wout),
        out_specs=[
            pl.BlockSpec((None, tm, d), lambda b, i: (b, i, 0)),
            pl.BlockSpec((None, CONV_WIDTH - 1, d), lambda b, i: (b, 0, 0)),
            pl.BlockSpec((None, 1, d), lambda b, i: (b, 0, 0)),
        ],
        out_shape=[
            jax.ShapeDtypeStruct((bsz, seq, d), F32),
            jax.ShapeDtypeStruct((bsz, CONV_WIDTH - 1, d), F32),
            jax.ShapeDtypeStruct((bsz, 1, d), F32),
        ],
        scratch_shapes=[
            pltpu.VMEM((tm + head, d), F32),
            pltpu.VMEM((head, d), F32),
            pltpu.VMEM((tm, d), F32),
            pltpu.VMEM((tm, d), F32),
            pltpu.VMEM((tm, d), F32),
            pltpu.VMEM((tm, d), F32),
            pltpu.VMEM((tm, d), F32),
            pltpu.VMEM((1, d), F32),
        ],
        compiler_params=_params(("arbitrary", "arbitrary")),
        name="rec_prompt",
    )(x, g.reshape(1, d), win, cw, cb.reshape(1, d), wrg, brg.reshape(1, d), wig,
      big.reshape(1, d), lam.reshape(1, d), wout)


def _rec_sample_kernel(x_ref, sc_ref, h0_ref, g_ref, win_ref, cw_ref, cb_ref, wrg_ref, brg_ref,
                       wig_ref, big_ref, lam_ref, wout_ref, y_ref, conv_ref, hout_ref,
                       a_scr, b_scr, hs_scr, *, t_new, nb):
    d = x_ref.shape[-1]
    x = x_ref[...]
    h = _rms(x, g_ref[...]).astype(BF16)
    gx = _dot(h, win_ref[...])
    gate = gx[:, :d]
    xb = gx[:, d:]
    xp = [sc_ref[j] for j in range(CONV_WIDTH - 1)] + [xb[t * nb:(t + 1) * nb, :] for t in range(t_new)]
    xcs = []
    for t in range(t_new):
        acc = cb_ref[...]
        for j in range(CONV_WIDTH):
            acc = acc + xp[t + j] * cw_ref[j:j + 1, :]
        xcs.append(acc)
    xc = jnp.concatenate(xcs, axis=0)

    _lru_coeffs(xc, wrg_ref, brg_ref, wig_ref, big_ref, lam_ref, a_scr, b_scr)

    hprev = h0_ref[...]
    for t in range(t_new):
        sl = slice(t * nb, (t + 1) * nb)
        hprev = a_scr[sl, :] * hprev + b_scr[sl, :]
        hs_scr[sl, :] = hprev
    y_ref[...] = x + _dot((hs_scr[...] * jax.nn.gelu(gate)).astype(BF16), wout_ref[...])
    for j in range(CONV_WIDTH - 1):
        conv_ref[j] = xp[t_new + j]
    hout_ref[...] = hprev


def _rec_sample(x, sc, h0, g, win, cw, cb, wrg, brg, wig, big, lam, wout, *, t_new):
    n, d = x.shape
    nb = n // t_new
    kern = functools.partial(_rec_sample_kernel, t_new=t_new, nb=nb)
    return pl.pallas_call(
        kern,
        grid=(1,),
        in_specs=[_full((n, d)), _full(sc.shape), _full(h0.shape)]
        + _rec_weight_specs(d, win, wrg, wig, wout),
        out_specs=[_full((n, d)), _full(sc.shape), _full(h0.shape)],
        out_shape=[
            jax.ShapeDtypeStruct((n, d), F32),
            jax.ShapeDtypeStruct(sc.shape, F32),
            jax.ShapeDtypeStruct(h0.shape, F32),
        ],
        scratch_shapes=[pltpu.VMEM((n, d), F32)] * 3,
        compiler_params=_params(("arbitrary",)),
        name="rec_sample",
    )(x, sc, h0, g.reshape(1, d), win, cw, cb.reshape(1, d), wrg, brg.reshape(1, d), wig,
      big.reshape(1, d), lam.reshape(1, d), wout)


def _pick_tile(seq, want):
    tm = min(want, seq)
    while seq % tm:
        tm //= 2
    return tm


def kernel(x_prompt, x_sample, cache_k, cache_v, state_conv, state_h, attn_norm, w_qkv, w_attn_out,
           attn_sinks, rec_norm, w_rec_in, conv_w, conv_b, w_rgate, b_rgate, w_igate, b_igate,
           lru_lambda, w_rec_out, ffn_norm, w_ffn_in, w_ffn_out, final_norm):
    bsz, seq, d = x_prompt.shape
    dbsz, t_new, _ = x_sample.shape
    depth = ffn_norm.shape[0]
    nh = attn_sinks.shape[1]
    hd = w_attn_out.shape[1] // nh
    nkv = (w_qkv.shape[2] // hd - nh) // 2
    nq, nk = nh * hd, nkv * hd
    win = cache_k.shape[2]
    tm = _pick_tile(seq, TILE_TOKENS)
    tm_s = dbsz * t_new
    bb = _pick_tile(dbsz, 8)

    xp = x_prompt
    xs = x_sample.reshape(dbsz * t_new, d)
    time_major = False
    nk_p, nv_p, nk_s, nv_s, nc_p, nh_p, nc_s, nh_s = [], [], [], [], [], [], [], []

    for layer in range(depth):
        j = layer // 2
        wfi = w_ffn_in[layer].astype(BF16)
        wfo = w_ffn_out[layer].astype(BF16)
        gf = final_norm if layer == depth - 1 else None
        if layer % 2 == 0:
            wqkv = w_qkv[j].astype(BF16)
            wo = w_attn_out[j].astype(BF16)
            xp, kp, vp = _attn_prompt(xp, attn_norm[j], wqkv, wo, attn_sinks[j],
                                      tm=tm, nh=nh, nkv=nkv, hd=hd)
            nk_p.append(kp.reshape(bsz, BLOCK, nkv, hd))
            nv_p.append(vp.reshape(bsz, BLOCK, nkv, hd))

            if time_major:
                xs = xs.reshape(t_new, dbsz, d).transpose(1, 0, 2).reshape(dbsz * t_new, d)
                time_major = False
            qkv = _rms_matmul(xs, attn_norm[j], wqkv)
            q3 = qkv[:, :nq].reshape(dbsz, t_new * nh, hd)
            kn3 = qkv[:, nq:nq + nk].reshape(dbsz, t_new, nk)
            vn3 = qkv[:, nq + nk:].reshape(dbsz, t_new, nk)
            ck = cache_k[j].reshape(dbsz, win, nk)
            cv = cache_v[j].reshape(dbsz, win, nk)
            o3 = _attn_sample(q3, kn3, vn3, ck, cv, attn_sinks[j], nh=nh, nkv=nkv, hd=hd, bb=bb)
            nk_s.append(jnp.concatenate([cache_k[j][:, t_new:], kn3.reshape(dbsz, t_new, nkv, hd)], axis=1))
            nv_s.append(jnp.concatenate([cache_v[j][:, t_new:], vn3.reshape(dbsz, t_new, nkv, hd)], axis=1))
            xs = _ffn(xs, ffn_norm[layer], wfi, wfo, tm=tm_s, o=o3.reshape(dbsz * t_new, nq), wo=wo, gf=gf)
        else:
            rec_w = (rec_norm[j], w_rec_in[j].astype(BF16), conv_w[j], conv_b[j],
                     w_rgate[j].astype(BF16), b_rgate[j], w_igate[j].astype(BF16), b_igate[j],
                     lru_lambda[j], w_rec_out[j].astype(BF16))
            xp, cp, hp = _rec_prompt(xp, *rec_w, tm=tm)
            nc_p.append(cp)
            nh_p.append(hp.reshape(bsz, d))

            if not time_major:
                xs = xs.reshape(dbsz, t_new, d).transpose(1, 0, 2).reshape(t_new * dbsz, d)
                time_major = True
            sc = state_conv[j].transpose(1, 0, 2)
            xs, cs, hs = _rec_sample(xs, sc, state_h[j], *rec_w, t_new=t_new)
            nc_s.append(cs.transpose(1, 0, 2))
            nh_s.append(hs)
            xs = _ffn(xs, ffn_norm[layer], wfi, wfo, tm=tm_s, gf=gf)
        xp = _ffn(xp.reshape(bsz * seq, d), ffn_norm[layer], wfi, wfo, tm=tm, gf=gf).reshape(bsz, seq, d)

    if time_major:
        ys = xs.reshape(t_new, dbsz, d).transpose(1, 0, 2)
    else:
        ys = xs.reshape(dbsz, t_new, d)
    return (xp, ys, jnp.stack(nk_p), jnp.stack(nv_p), jnp.stack(nk_s), jnp.stack(nv_s),
            jnp.stack(nc_p), jnp.stack(nh_p), jnp.stack(nc_s), jnp.stack(nh_s))
```

```python
import functools

import numpy as np

import jax
import jax.numpy as jnp
from jax import lax
from jax.experimental import pallas as pl
from jax.experimental.pallas import tpu as pltpu

EPS = 1e-6
NEG_INF = -1e30
LRU_C = 8.0
CONV_WIDTH = 4
N_LRU_BLOCKS = 4
BLOCK = 128
SUBLANES = 8
LANES = 128
SM_ROWS = 64
TILE_TOKENS = 512
FFN_TILE_TOKENS = 1024
LOG2E = 1.4426950408889634
VMEM_LIMIT = 56 * 1024 * 1024

F32 = jnp.float32
BF16 = jnp.bfloat16


def _rms(x, g):
    return x * lax.rsqrt(jnp.mean(x * x, axis=-1, keepdims=True) + EPS) * g


def _dot(a, b):
    return jnp.dot(a, b, preferred_element_type=F32)


def _dot_nt(a, b):
    return lax.dot_general(a, b, (((1,), (1,)), ((), ())), preferred_element_type=F32)


def _softplus(z):
    return jnp.maximum(z, 0.0) + jnp.log1p(jnp.exp(-jnp.abs(z)))


def _full(shape):
    nd = len(shape)
    return pl.BlockSpec(shape, lambda *_: (0,) * nd, pipeline_mode=pl.Buffered(1))


def _whole(shape):
    nd = len(shape)
    return pl.BlockSpec(shape, lambda *_: (0,) * nd)


def _params(sem):
    return pltpu.CompilerParams(dimension_semantics=sem, vmem_limit_bytes=VMEM_LIMIT)


def _attn_prompt_kernel(x_ref, g_ref, wqkv_ref, wo_ref, bias_ref, sink_ref,
                        y_ref, kout_ref, vout_ref,
                        q_scr, klo_scr, khi_scr, vlo_scr, vhi_scr, s_scr, m_scr, p_scr, inv_scr, o_scr,
                        *, tm, nh, nkv, hd):
    i = pl.program_id(1)
    last = pl.num_programs(1) - 1
    grp = nh // nkv
    nq, nk = nh * hd, nkv * hd
    half = (grp // 2) * BLOCK
    unit = 2 * half
    nblk = tm // BLOCK

    x = x_ref[...]
    h = _rms(x, g_ref[...]).astype(BF16)
    qkv = _dot(h, wqkv_ref[...])
    q_scr[...] = (qkv[:, :nq] * (hd ** -0.5 * LOG2E)).astype(BF16)
    k = qkv[:, nq:nq + nk]
    v = qkv[:, nq + nk:]

    @pl.when(i == 0)
    def _():
        for scr in (klo_scr, khi_scr, vlo_scr, vhi_scr):
            scr[0:BLOCK, :] = jnp.zeros((BLOCK, nkv * LANES), BF16)

    @pl.when(i == last)
    def _():
        kout_ref[...] = k[tm - BLOCK:, :]
        vout_ref[...] = v[tm - BLOCK:, :]

    lo = lax.broadcasted_iota(jnp.int32, (1, LANES), 1) < hd
    rows = slice(BLOCK, BLOCK + tm)
    for c in range(nk // LANES):
        for src, lo_scr, hi_scr in ((k, klo_scr, khi_scr), (v, vlo_scr, vhi_scr)):
            chunk = src[:, c * LANES:(c + 1) * LANES]
            swapped = pltpu.roll(chunk, hd, axis=1)
            even = slice(2 * c * LANES, (2 * c + 1) * LANES)
            odd = slice((2 * c + 1) * LANES, (2 * c + 2) * LANES)
            lo_scr[rows, even] = jnp.where(lo, chunk, 0.0).astype(BF16)
            hi_scr[rows, even] = jnp.where(lo, 0.0, swapped).astype(BF16)
            lo_scr[rows, odd] = jnp.where(lo, swapped, 0.0).astype(BF16)
            hi_scr[rows, odd] = jnp.where(lo, 0.0, chunk).astype(BF16)

    def scores(j, buf):
        r0 = j * BLOCK
        for kv in range(nkv):
            qp = jnp.concatenate(
                [q_scr[r0:r0 + BLOCK, (kv * grp + 2 * pr) * hd:(kv * grp + 2 * pr + 2) * hd]
                 for pr in range(grp // 2)], axis=0)
            win = (slice(r0, r0 + 2 * BLOCK), slice(kv * LANES, (kv + 1) * LANES))
            s_scr[buf, kv * unit:kv * unit + half, :] = _dot_nt(qp, klo_scr[win])
            s_scr[buf, kv * unit + half:(kv + 1) * unit, :] = _dot_nt(qp, khi_scr[win])

    def softmax(j, buf):
        sel = jnp.where(i == 0, 1, 0) if j == 0 else 0
        for c in range(nkv * unit // SM_ROWS):
            rs = slice(c * SM_ROWS, (c + 1) * SM_ROWS)
            t = s_scr[buf, rs, :] + bias_ref[sel, rs, :]
            m_scr[rs, :] = jnp.maximum(jnp.max(t, axis=-1, keepdims=True), sink_ref[rs, :])
        for c in range(nkv * unit // SM_ROWS):
            rs = slice(c * SM_ROWS, (c + 1) * SM_ROWS)
            t = s_scr[buf, rs, :] + bias_ref[sel, rs, :]
            m = m_scr[rs, :]
            p = jnp.concatenate([jnp.exp2(t[:, q * LANES:(q + 1) * LANES] - m)
                                 for q in range(2 * BLOCK // LANES)], axis=1)
            p_scr[buf, rs, :] = p.astype(BF16)
            denom = jnp.sum(p, axis=-1, keepdims=True) + jnp.exp2(sink_ref[rs, :] - m)
            inv_scr[rs, :] = 1.0 / denom

    def values(j, buf):
        r0 = j * BLOCK
        for kv in range(nkv):
            win = (slice(r0, r0 + 2 * BLOCK), slice(kv * LANES, (kv + 1) * LANES))
            ev = slice(kv * unit, kv * unit + half)
            od = slice(kv * unit + half, (kv + 1) * unit)
            o = _dot(p_scr[buf, ev, :], vlo_scr[win]) + _dot(p_scr[buf, od, :], vhi_scr[win])
            o = o * jnp.where(lo, inv_scr[ev, :], inv_scr[od, :])
            for pr in range(grp // 2):
                c0 = (kv * grp + 2 * pr) * hd
                o_scr[r0:r0 + BLOCK, c0:c0 + LANES] = o[pr * BLOCK:(pr + 1) * BLOCK, :].astype(BF16)

    scores(0, 0)
    for j in range(nblk):
        if j + 1 < nblk:
            scores(j + 1, (j + 1) % 2)
        softmax(j, j % 2)
        values(j, j % 2)

    y_ref[...] = x + _dot(o_scr[...], wo_ref[...])

    for scr in (klo_scr, khi_scr, vlo_scr, vhi_scr):
        scr[0:BLOCK, :] = scr[tm:tm + BLOCK, :]


def _alibi_slopes(nh):
    return (2.0 ** (-8.0 * np.arange(1, nh + 1, dtype=np.float32) / nh)).astype(np.float32)


def _attn_prompt(x, g, wqkv, wo, sinks, *, tm, nh, nkv, hd):
    bsz, seq, d = x.shape
    grp = nh // nkv
    nk = nkv * hd
    assert 2 * hd == LANES and grp % 2 == 0 and tm % BLOCK == 0
    half = grp // 2 * BLOCK
    r = np.arange(grp * BLOCK)
    head_in_kv = 2 * ((r % half) // BLOCK) + r // half
    head = (np.arange(nkv)[:, None] * grp + head_in_kv[None, :]).reshape(-1)
    nrows = nkv * grp * BLOCK
    qi = np.tile(np.arange(BLOCK), nkv * grp)[:, None]
    si = np.arange(2 * BLOCK)[None, :]
    dist = qi - si + BLOCK
    band = (dist >= 0) & (dist <= BLOCK)
    slopes = _alibi_slopes(nh)[head][:, None]
    bias = np.where(band, -slopes * dist.astype(np.float32) * np.float32(LOG2E), np.float32(NEG_INF))
    bias = np.stack([bias, np.where(si < BLOCK, np.float32(NEG_INF), bias)]).astype(np.float32)
    sink = jnp.broadcast_to((sinks.astype(F32) * LOG2E)[head][:, None], (nrows, LANES))

    kern = functools.partial(_attn_prompt_kernel, tm=tm, nh=nh, nkv=nkv, hd=hd)
    return pl.pallas_call(
        kern,
        grid=(bsz, seq // tm),
        in_specs=[
            pl.BlockSpec((None, tm, d), lambda b, i: (b, i, 0)),
            _full((1, d)),
            _full(wqkv.shape),
            _full(wo.shape),
            _full(bias.shape),
            _full(sink.shape),
        ],
        out_specs=[
            pl.BlockSpec((None, tm, d), lambda b, i: (b, i, 0)),
            pl.BlockSpec((None, BLOCK, nk), lambda b, i: (b, 0, 0)),
            pl.BlockSpec((None, BLOCK, nk), lambda b, i: (b, 0, 0)),
        ],
        out_shape=[
            jax.ShapeDtypeStruct((bsz, seq, d), F32),
            jax.ShapeDtypeStruct((bsz, BLOCK, nk), F32),
            jax.ShapeDtypeStruct((bsz, BLOCK, nk), F32),
        ],
        scratch_shapes=[
            pltpu.VMEM((tm, nh * hd), BF16),
            pltpu.VMEM((tm + BLOCK, nkv * LANES), BF16),
            pltpu.VMEM((tm + BLOCK, nkv * LANES), BF16),
            pltpu.VMEM((tm + BLOCK, nkv * LANES), BF16),
            pltpu.VMEM((tm + BLOCK, nkv * LANES), BF16),
            pltpu.VMEM((2, nrows, 2 * BLOCK), F32),
            pltpu.VMEM((nrows, LANES), F32),
            pltpu.VMEM((2, nrows, 2 * BLOCK), BF16),
            pltpu.VMEM((nrows, LANES), F32),
            pltpu.VMEM((tm, nh * hd), BF16),
        ],
        compiler_params=_params(("arbitrary", "arbitrary")),
        name="attn_prompt",
    )(x, g.reshape(1, d), wqkv, wo, bias, sink)


def _rms_matmul_kernel(x_ref, g_ref, w_ref, y_ref):
    h = _rms(x_ref[...], g_ref[...]).astype(BF16)
    y_ref[...] = _dot(h, w_ref[...])


def _rms_matmul(x, g, w):
    n, d = x.shape
    return pl.pallas_call(
        _rms_matmul_kernel,
        grid=(1,),
        in_specs=[_full((n, d)), _full((1, d)), _full(w.shape)],
        out_specs=_whole((n, w.shape[1])),
        out_shape=jax.ShapeDtypeStruct((n, w.shape[1]), F32),
        compiler_params=_params(("arbitrary",)),
        name="rms_matmul",
    )(x, g.reshape(1, d), w)


def _attn_sample_kernel(q_ref, kn_ref, vn_ref, ck_ref, cv_ref, e_ref, et_ref, mask_ref,
                        bias_ref, sink_ref, o_ref, qx_scr, kk_scr, vv_scr, s_scr, p_scr, ox_scr,
                        *, bb, rows, t_new, hd):
    win = ck_ref.shape[1]

    @pl.when(pl.program_id(0) == 0)
    def _():
        kk_scr[:, win:, :] = jnp.zeros((bb, win, kk_scr.shape[2]), BF16)
        vv_scr[:, win:, :] = jnp.zeros((bb, win, vv_scr.shape[2]), BF16)

    q = (q_ref[...].reshape(bb * rows, hd) * (hd ** -0.5 * LOG2E)).astype(BF16)
    mask = mask_ref[...]
    qx_scr[...] = (_dot(q, e_ref[...]) * mask).astype(BF16)
    for b in range(bb):
        kk_scr[b, 0:win, :] = ck_ref[b].astype(BF16)
        kk_scr[b, win:win + t_new, :] = kn_ref[b].astype(BF16)
        vv_scr[b, 0:win, :] = cv_ref[b].astype(BF16)
        vv_scr[b, win:win + t_new, :] = vn_ref[b].astype(BF16)
    for b in range(bb):
        rs = slice(b * rows, (b + 1) * rows)
        s_scr[rs, :] = _dot_nt(qx_scr[rs, :], kk_scr[b])

    t = s_scr[...] + bias_ref[...]
    sink = sink_ref[...]
    m = jnp.maximum(jnp.max(t, axis=-1, keepdims=True), sink)
    p = jnp.concatenate([jnp.exp2(t[:, c * LANES:(c + 1) * LANES] - m)
                         for c in range(2 * win // LANES)], axis=1)
    p_scr[...] = p.astype(BF16)
    inv = 1.0 / (jnp.sum(p, axis=-1, keepdims=True) + jnp.exp2(sink - m))
    scale = jnp.concatenate([inv] * (mask.shape[1] // LANES), axis=1) * mask
    for b in range(bb):
        rs = slice(b * rows, (b + 1) * rows)
        ox_scr[rs, :] = (_dot(p_scr[rs, :], vv_scr[b]) * scale[rs, :]).astype(BF16)
    o_ref[...] = _dot(ox_scr[...], et_ref[...]).reshape(bb, rows, hd)


def _attn_sample(q3, kn3, vn3, ck, cv, sinks, *, nh, nkv, hd, bb):
    dbsz, rows, _ = q3.shape
    t_new = kn3.shape[1]
    win = ck.shape[1]
    nk = nkv * hd
    grp = nh // nkv
    assert t_new <= win and win % LANES == 0 and nk % LANES == 0
    row_t = np.arange(rows)[:, None] // nh
    row_h = np.arange(rows) % nh
    row_slope = _alibi_slopes(nh)[row_h][:, None]
    key = np.arange(2 * win)[None, :]
    dist = (row_t - key + win).astype(np.float32)
    valid = (key >= row_t) & (key <= row_t + win) & (key < win + t_new)
    bias = np.where(valid, -row_slope * dist * np.float32(LOG2E), np.float32(NEG_INF)).astype(np.float32)
    bias = np.tile(bias, (bb, 1))
    mask = (np.arange(nk)[None, :] // hd == (row_h // grp)[:, None]).astype(np.float32)
    mask = np.tile(mask, (bb, 1))
    e = np.tile(np.eye(hd, dtype=np.float32), (1, nkv))
    sink = jnp.broadcast_to((sinks.astype(F32) * LOG2E)[np.tile(row_h, bb)][:, None], (bb * rows, LANES))

    kern = functools.partial(_attn_sample_kernel, bb=bb, rows=rows, t_new=t_new, hd=hd)
    return pl.pallas_call(
        kern,
        grid=(dbsz // bb,),
        in_specs=[
            pl.BlockSpec((bb, rows, hd), lambda i: (i, 0, 0)),
            pl.BlockSpec((bb, t_new, nk), lambda i: (i, 0, 0)),
            pl.BlockSpec((bb, t_new, nk), lambda i: (i, 0, 0)),
            pl.BlockSpec((bb, win, nk), lambda i: (i, 0, 0)),
            pl.BlockSpec((bb, win, nk), lambda i: (i, 0, 0)),
            _full(e.shape), _full(e.T.shape), _full(mask.shape),
            _full(bias.shape), _full(sink.shape),
        ],
        out_specs=pl.BlockSpec((bb, rows, hd), lambda i: (i, 0, 0)),
        out_shape=jax.ShapeDtypeStruct((dbsz, rows, hd), F32),
        scratch_shapes=[
            pltpu.VMEM((bb * rows, nk), BF16),
            pltpu.VMEM((bb, 2 * win, nk), BF16),
            pltpu.VMEM((bb, 2 * win, nk), BF16),
            pltpu.VMEM((bb * rows, 2 * win), F32),
            pltpu.VMEM((bb * rows, 2 * win), BF16),
            pltpu.VMEM((bb * rows, nk), BF16),
        ],
        compiler_params=_params(("arbitrary",)),
        name="attn_sample",
    )(q3, kn3, vn3, ck, cv, jnp.asarray(e, BF16), jnp.asarray(e.T, BF16), mask, bias, sink)


def _ffn_kernel(*refs, dff, chunk, proj, final):
    refs = list(refs)
    x_ref = refs.pop(0)
    if proj:
        o_ref, wo_ref = refs.pop(0), refs.pop(0)
    g_ref, win_ref, wout_ref = refs.pop(0), refs.pop(0), refs.pop(0)
    if final:
        gf_ref = refs.pop(0)
    y_ref, a_scr = refs

    x = x_ref[...]
    if proj:
        x = x + _dot(o_ref[...].astype(BF16), wo_ref[...])
    h = _rms(x, g_ref[...]).astype(BF16)
    for c in range(dff // chunk):
        gate = _dot(h, win_ref[:, c * chunk:(c + 1) * chunk])
        up = _dot(h, win_ref[:, dff + c * chunk:dff + (c + 1) * chunk])
        a_scr[:, c * chunk:(c + 1) * chunk] = (gate * jax.nn.sigmoid(gate) * up).astype(BF16)
    y = x + _dot(a_scr[...], wout_ref[...])
    if final:
        y = _rms(y, gf_ref[...])
    y_ref[...] = y


def _ffn(x, g, win, wout, *, tm, o=None, wo=None, gf=None, chunk=256):
    n, d = x.shape
    dff = wout.shape[0]
    proj, final = o is not None, gf is not None
    tile = pl.BlockSpec((tm, d), lambda i: (i, 0))
    args, specs = [x], [tile]
    if proj:
        args += [o, wo]
        specs += [pl.BlockSpec((tm, o.shape[1]), lambda i: (i, 0)), _full(wo.shape)]
    args += [g.reshape(1, d), win, wout]
    specs += [_full((1, d)), _full(win.shape), _full(wout.shape)]
    if final:
        args.append(gf.reshape(1, d))
        specs.append(_full((1, d)))
    kern = functools.partial(_ffn_kernel, dff=dff, chunk=chunk, proj=proj, final=final)
    return pl.pallas_call(
        kern,
        grid=(n // tm,),
        in_specs=specs,
        out_specs=tile,
        out_shape=jax.ShapeDtypeStruct((n, d), F32),
        scratch_shapes=[pltpu.VMEM((tm, dff), BF16)],
        compiler_params=_params(("arbitrary",)),
        name="ffn",
    )(*args)


def _lru_coeffs(xc, wrg_ref, brg_ref, wig_ref, big_ref, lam_ref, a_out, b_out):
    d = xc.shape[1]
    blk = d // N_LRU_BLOCKS
    for n in range(N_LRU_BLOCKS):
        sl = slice(n * blk, (n + 1) * blk)
        xcn = xc[:, sl]
        xcb = xcn.astype(BF16)
        r = jax.nn.sigmoid(_dot(xcb, wrg_ref[n]) + brg_ref[:, sl])
        ig = jax.nn.sigmoid(_dot(xcb, wig_ref[n]) + big_ref[:, sl])
        log_a = -LRU_C * r * _softplus(-lam_ref[:, sl])
        a = jnp.exp(log_a)
        a_out[:, sl] = a
        b_out[:, sl] = jnp.sqrt(-jnp.tanh(log_a) * (a * a + 1.0)) * (ig * xcn)


def _rec_prompt_kernel(x_ref, g_ref, win_ref, cw_ref, cb_ref, wrg_ref, brg_ref, wig_ref, big_ref,
                       lam_ref, wout_ref, y_ref, conv_ref, hout_ref,
                       xb_scr, tail_scr, gate_scr, a_scr, b_scr, hl_scr, pl_scr, carry_scr, *, tm):
    i = pl.program_id(1)
    last = pl.num_programs(1) - 1
    d = x_ref.shape[-1]
    ns = tm // SUBLANES
    npre = CONV_WIDTH - 1
    head = npre * SUBLANES

    @pl.when(i == 0)
    def _():
        tail_scr[...] = jnp.zeros((head, d), F32)
        carry_scr[...] = jnp.zeros((1, d), F32)

    x = x_ref[...].reshape(SUBLANES, ns, d).swapaxes(0, 1).reshape(tm, d)
    h = _rms(x, g_ref[...]).astype(BF16)
    gx = _dot(h, win_ref[...])
    gate_scr[...] = jax.nn.gelu(gx[:, :d])
    xb_scr[head:head + tm, :] = gx[:, d:]

    seg = lax.broadcasted_iota(jnp.int32, (SUBLANES, 1), 0)
    for k in range(npre):
        rows = slice(k * SUBLANES, (k + 1) * SUBLANES)
        own = pltpu.roll(xb_scr[tm + k * SUBLANES:tm + (k + 1) * SUBLANES, :], 1, axis=0)
        prev = pltpu.roll(tail_scr[rows, :], 1, axis=0)
        xb_scr[rows, :] = jnp.where(seg == 0, prev, own)
    tail_scr[...] = xb_scr[tm:tm + head, :]

    xc = cb_ref[...]
    for j in range(CONV_WIDTH):
        xc = xc + xb_scr[j * SUBLANES:j * SUBLANES + tm, :] * cw_ref[j:j + 1, :]

    _lru_coeffs(xc, wrg_ref, brg_ref, wig_ref, big_ref, lam_ref, a_scr, b_scr)

    def body(s, carry):
        hloc, prod = carry
        rows = pl.ds(pl.multiple_of(s * SUBLANES, SUBLANES), SUBLANES)
        a = a_scr[rows, :]
        hloc = a * hloc + b_scr[rows, :]
        prod = a * prod
        hl_scr[rows, :] = hloc
        pl_scr[rows, :] = prod
        return hloc, prod

    hend, pend = lax.fori_loop(0, ns, body, (jnp.zeros((SUBLANES, d), F32), jnp.ones((SUBLANES, d), F32)),
                               unroll=4)

    a = jnp.where(seg == 0, 0.0, pltpu.roll(pend, 1, axis=0))
    b = jnp.where(seg == 0, carry_scr[...], pltpu.roll(hend, 1, axis=0))
    shift = 1
    while shift < SUBLANES:
        keep = seg >= shift
        b = jnp.where(keep, a * pltpu.roll(b, shift, axis=0) + b, b)
        a = jnp.where(keep, a * pltpu.roll(a, shift, axis=0), a)
        shift *= 2
    hstart = b
    hfinal = hend + pend * hstart
    carry = hfinal[SUBLANES - 1:SUBLANES, :]
    carry_scr[...] = carry

    hs = hl_scr[...].reshape(ns, SUBLANES, d) + pl_scr[...].reshape(ns, SUBLANES, d) * hstart[None]
    y = _dot((hs.reshape(tm, d) * gate_scr[...]).astype(BF16), wout_ref[...])
    y_ref[...] = x_ref[...] + y.reshape(ns, SUBLANES, d).swapaxes(0, 1).reshape(tm, d)

    @pl.when(i == last)
    def _():
        for k in range(npre):
            r = head + (ns - npre + k) * SUBLANES + SUBLANES - 1
            conv_ref[k:k + 1, :] = xb_scr[r:r + 1, :]
        hout_ref[...] = carry


def _rec_weight_specs(d, win, wrg, wig, wout):
    return [_full((1, d)), _full(win.shape), _full((CONV_WIDTH, d)), _full((1, d)),
            _full(wrg.shape), _full((1, d)), _full(wig.shape), _full((1, d)), _full((1, d)),
            _full(wout.shape)]


def _rec_prompt(x, g, win, cw, cb, wrg, brg, wig, big, lam, wout, *, tm):
    bsz, seq, d = x.shape
    kern = functools.partial(_rec_prompt_kernel, tm=tm)
    head = (CONV_WIDTH - 1) * SUBLANES
    assert tm // SUBLANES >= CONV_WIDTH - 1
    return pl.pallas_call(
        kern,
        grid=(bsz, seq // tm),
        in_specs=[pl.BlockSpec((None, tm, d), lambda b, i: (b, i, 0))]
        + _rec_weight_specs(d, win, wrg, wig, wout),
        out_specs=[
            pl.BlockSpec((None, tm, d), lambda b, i: (b, i, 0)),
            pl.BlockSpec((None, CONV_WIDTH - 1, d), lambda b, i: (b, 0, 0)),
            pl.BlockSpec((None, 1, d), lambda b, i: (b, 0, 0)),
        ],
        out_shape=[
            jax.ShapeDtypeStruct((bsz, seq, d), F32),
            jax.ShapeDtypeStruct((bsz, CONV_WIDTH - 1, d), F32),
            jax.ShapeDtypeStruct((bsz, 1, d), F32),
        ],
        scratch_shapes=[
            pltpu.VMEM((tm + head, d), F32),
            pltpu.VMEM((head, d), F32),
            pltpu.VMEM((tm, d), F32),
            pltpu.VMEM((tm, d), F32),
            pltpu.VMEM((tm, d), F32),
            pltpu.VMEM((tm, d), F32),
            pltpu.VMEM((tm, d), F32),
            pltpu.VMEM((1, d), F32),
        ],
        compiler_params=_params(("arbitrary", "arbitrary")),
        name="rec_prompt",
    )(x, g.reshape(1, d), win, cw, cb.reshape(1, d), wrg, brg.reshape(1, d), wig,
      big.reshape(1, d), lam.reshape(1, d), wout)


def _rec_sample_kernel(x_ref, sc_ref, h0_ref, g_ref, win_ref, cw_ref, cb_ref, wrg_ref, brg_ref,
                       wig_ref, big_ref, lam_ref, wout_ref, y_ref, conv_ref, hout_ref,
                       a_scr, b_scr, hs_scr, *, t_new, nb):
    d = x_ref.shape[-1]
    x = x_ref[...]
    h = _rms(x, g_ref[...]).astype(BF16)
    gx = _dot(h, win_ref[...])
    gate = gx[:, :d]
    xb = gx[:, d:]
    xp = [sc_ref[j] for j in range(CONV_WIDTH - 1)] + [xb[t * nb:(t + 1) * nb, :] for t in range(t_new)]
    xcs = []
    for t in range(t_new):
        acc = cb_ref[...]
        for j in range(CONV_WIDTH):
            acc = acc + xp[t + j] * cw_ref[j:j + 1, :]
        xcs.append(acc)
    xc = jnp.concatenate(xcs, axis=0)

    _lru_coeffs(xc, wrg_ref, brg_ref, wig_ref, big_ref, lam_ref, a_scr, b_scr)

    hprev = h0_ref[...]
    for t in range(t_new):
        sl = slice(t * nb, (t + 1) * nb)
        hprev = a_scr[sl, :] * hprev + b_scr[sl, :]
        hs_scr[sl, :] = hprev
    y_ref[...] = x + _dot((hs_scr[...] * jax.nn.gelu(gate)).astype(BF16), wout_ref[...])
    for j in range(CONV_WIDTH - 1):
        conv_ref[j] = xp[t_new + j]
    hout_ref[...] = hprev


def _rec_sample(x, sc, h0, g, win, cw, cb, wrg, brg, wig, big, lam, wout, *, t_new):
    n, d = x.shape
    nb = n // t_new
    kern = functools.partial(_rec_sample_kernel, t_new=t_new, nb=nb)
    return pl.pallas_call(
        kern,
        grid=(1,),
        in_specs=[_full((n, d)), _full(sc.shape), _full(h0.shape)]
        + _rec_weight_specs(d, win, wrg, wig, wout),
        out_specs=[_whole((n, d)), _whole(sc.shape), _whole(h0.shape)],
        out_shape=[
            jax.ShapeDtypeStruct((n, d), F32),
            jax.ShapeDtypeStruct(sc.shape, F32),
            jax.ShapeDtypeStruct(h0.shape, F32),
        ],
        scratch_shapes=[pltpu.VMEM((n, d), F32)] * 3,
        compiler_params=_params(("arbitrary",)),
        name="rec_sample",
    )(x, sc, h0, g.reshape(1, d), win, cw, cb.reshape(1, d), wrg, brg.reshape(1, d), wig,
      big.reshape(1, d), lam.reshape(1, d), wout)


def _shift_append(cache, new):
    t = new.shape[1]
    pad = [(0, 0, 0)] * cache.ndim
    pad[1] = (-t, t, 0)
    shifted = lax.pad(cache, jnp.zeros((), cache.dtype), pad)
    return lax.dynamic_update_slice_in_dim(shifted, new.astype(cache.dtype), cache.shape[1] - t, axis=1)


def _pick_tile(seq, want):
    tm = min(want, seq)
    while seq % tm:
        tm //= 2
    return tm


def kernel(x_prompt, x_sample, cache_k, cache_v, state_conv, state_h, attn_norm, w_qkv, w_attn_out,
           attn_sinks, rec_norm, w_rec_in, conv_w, conv_b, w_rgate, b_rgate, w_igate, b_igate,
           lru_lambda, w_rec_out, ffn_norm, w_ffn_in, w_ffn_out, final_norm):
    bsz, seq, d = x_prompt.shape
    dbsz, t_new, _ = x_sample.shape
    depth = ffn_norm.shape[0]
    nh = attn_sinks.shape[1]
    hd = w_attn_out.shape[1] // nh
    nkv = (w_qkv.shape[2] // hd - nh) // 2
    nq, nk = nh * hd, nkv * hd
    win = cache_k.shape[2]
    tm = _pick_tile(seq, TILE_TOKENS)
    tm_ffn = _pick_tile(bsz * seq, FFN_TILE_TOKENS)
    tm_s = dbsz * t_new
    bb = _pick_tile(dbsz, 8)

    xp = x_prompt
    xs = x_sample.reshape(dbsz * t_new, d)
    time_major = False
    nk_p, nv_p, nk_s, nv_s, nc_p, nh_p, nc_s, nh_s = [], [], [], [], [], [], [], []

    for layer in range(depth):
        j = layer // 2
        wfi = w_ffn_in[layer].astype(BF16)
        wfo = w_ffn_out[layer].astype(BF16)
        gf = final_norm if layer == depth - 1 else None
        if layer % 2 == 0:
            wqkv = w_qkv[j].astype(BF16)
            wo = w_attn_out[j].astype(BF16)
            xp, kp, vp = _attn_prompt(xp, attn_norm[j], wqkv, wo, attn_sinks[j],
                                      tm=tm, nh=nh, nkv=nkv, hd=hd)
            nk_p.append(kp.reshape(bsz, BLOCK, nkv, hd))
            nv_p.append(vp.reshape(bsz, BLOCK, nkv, hd))

            if time_major:
                xs = xs.reshape(t_new, dbsz, d).transpose(1, 0, 2).reshape(dbsz * t_new, d)
                time_major = False
            qkv = _rms_matmul(xs, attn_norm[j], wqkv)
            q3 = qkv[:, :nq].reshape(dbsz, t_new * nh, hd)
            kn3 = qkv[:, nq:nq + nk].reshape(dbsz, t_new, nk)
            vn3 = qkv[:, nq + nk:].reshape(dbsz, t_new, nk)
            ck = cache_k[j].reshape(dbsz, win, nk)
            cv = cache_v[j].reshape(dbsz, win, nk)
            o3 = _attn_sample(q3, kn3, vn3, ck, cv, attn_sinks[j], nh=nh, nkv=nkv, hd=hd, bb=bb)
            nk_s.append(_shift_append(cache_k[j], kn3.reshape(dbsz, t_new, nkv, hd)))
            nv_s.append(_shift_append(cache_v[j], vn3.reshape(dbsz, t_new, nkv, hd)))
            xs = _ffn(xs, ffn_norm[layer], wfi, wfo, tm=tm_s, o=o3.reshape(dbsz * t_new, nq), wo=wo, gf=gf)
        else:
            rec_w = (rec_norm[j], w_rec_in[j].astype(BF16), conv_w[j], conv_b[j],
                     w_rgate[j].astype(BF16), b_rgate[j], w_igate[j].astype(BF16), b_igate[j],
                     lru_lambda[j], w_rec_out[j].astype(BF16))
            xp, cp, hp = _rec_prompt(xp, *rec_w, tm=tm)
            nc_p.append(cp)
            nh_p.append(hp.reshape(bsz, d))

            if not time_major:
                xs = xs.reshape(dbsz, t_new, d).transpose(1, 0, 2).reshape(t_new * dbsz, d)
                time_major = True
            sc = state_conv[j].transpose(1, 0, 2)
            xs, cs, hs = _rec_sample(xs, sc, state_h[j], *rec_w, t_new=t_new)
            nc_s.append(cs.transpose(1, 0, 2))
            nh_s.append(hs)
            xs = _ffn(xs, ffn_norm[layer], wfi, wfo, tm=tm_s, gf=gf)
        xp = _ffn(xp.reshape(bsz * seq, d), ffn_norm[layer], wfi, wfo, tm=tm_ffn, gf=gf).reshape(bsz, seq, d)

    if time_major:
        ys = xs.reshape(t_new, dbsz, d).transpose(1, 0, 2)
    else:
        ys = xs.reshape(dbsz, t_new, d)
    return (xp, ys, jnp.stack(nk_p), jnp.stack(nv_p), jnp.stack(nk_s), jnp.stack(nv_s),
            jnp.stack(nc_p), jnp.stack(nh_p), jnp.stack(nc_s), jnp.stack(nh_s))
```

```python
import functools

import numpy as np

import jax
import jax.numpy as jnp
from jax import lax
from jax.experimental import pallas as pl
from jax.experimental.pallas import tpu as pltpu

EPS = 1e-6
NEG_INF = -1e30
LRU_C = 8.0
CONV_WIDTH = 4
N_LRU_BLOCKS = 4
BLOCK = 128
SUBLANES = 8
LANES = 128
SM_ROWS = 64
TILE_TOKENS = 512
FFN_TILE_TOKENS = 1024
LOG2E = 1.4426950408889634
VMEM_LIMIT = 56 * 1024 * 1024

F32 = jnp.float32
BF16 = jnp.bfloat16


def _rms(x, g):
    return x * lax.rsqrt(jnp.mean(x * x, axis=-1, keepdims=True) + EPS) * g


def _dot(a, b):
    return jnp.dot(a, b, preferred_element_type=F32)


def _dot_nt(a, b):
    return lax.dot_general(a, b, (((1,), (1,)), ((), ())), preferred_element_type=F32)


def _softplus(z):
    return jnp.maximum(z, 0.0) + jnp.log1p(jnp.exp(-jnp.abs(z)))


def _full(shape):
    nd = len(shape)
    return pl.BlockSpec(shape, lambda *_: (0,) * nd, pipeline_mode=pl.Buffered(1))


def _whole(shape):
    nd = len(shape)
    return pl.BlockSpec(shape, lambda *_: (0,) * nd)


def _params(sem):
    return pltpu.CompilerParams(dimension_semantics=sem, vmem_limit_bytes=VMEM_LIMIT)


def _attn_prompt_kernel(x_ref, g_ref, wqkv_ref, wo_ref, bias_ref, sink_ref, sinkp_ref,
                        y_ref, kout_ref, vout_ref,
                        q_scr, klo_scr, khi_scr, vlo_scr, vhi_scr, t_scr, m_scr, p_scr, o_scr,
                        *, tm, nh, nkv, hd):
    i = pl.program_id(1)
    last = pl.num_programs(1) - 1
    grp = nh // nkv
    nq, nk = nh * hd, nkv * hd
    half = (grp // 2) * BLOCK
    unit = 2 * half
    nblk = tm // BLOCK
    lo = lax.broadcasted_iota(jnp.int32, (1, LANES), 1) < hd

    @pl.when(i == 0)
    def _():
        ones_lo = jnp.broadcast_to(jnp.where(lo, 1.0, 0.0).astype(BF16), (tm + BLOCK, LANES))
        ones_hi = jnp.broadcast_to(jnp.where(lo, 0.0, 1.0).astype(BF16), (tm + BLOCK, LANES))
        for kv in range(nkv):
            klo_scr[0:BLOCK, kv * LANES:(kv + 1) * LANES] = jnp.zeros((BLOCK, LANES), BF16)
            khi_scr[0:BLOCK, kv * LANES:(kv + 1) * LANES] = jnp.zeros((BLOCK, LANES), BF16)
            vlo_scr[0:BLOCK, 2 * kv * LANES:(2 * kv + 1) * LANES] = jnp.zeros((BLOCK, LANES), BF16)
            vhi_scr[0:BLOCK, 2 * kv * LANES:(2 * kv + 1) * LANES] = jnp.zeros((BLOCK, LANES), BF16)
            vlo_scr[:, (2 * kv + 1) * LANES:(2 * kv + 2) * LANES] = ones_lo
            vhi_scr[:, (2 * kv + 1) * LANES:(2 * kv + 2) * LANES] = ones_hi

    h = _rms(x_ref[...], g_ref[...]).astype(BF16)
    qkv = _dot(h, wqkv_ref[...])
    q_scr[...] = (qkv[:, :nq] * (hd ** -0.5 * LOG2E)).astype(BF16)
    k = qkv[:, nq:nq + nk]
    v = qkv[:, nq + nk:]

    @pl.when(i == last)
    def _():
        kout_ref[...] = k[tm - BLOCK:, :]
        vout_ref[...] = v[tm - BLOCK:, :]

    rows = slice(BLOCK, BLOCK + tm)
    for c in range(nk // LANES):
        for src, lo_scr, hi_scr, step in ((k, klo_scr, khi_scr, 1), (v, vlo_scr, vhi_scr, 2)):
            chunk = src[:, c * LANES:(c + 1) * LANES]
            swapped = pltpu.roll(chunk, hd, axis=1)
            even = slice(2 * c * step * LANES, (2 * c * step + 1) * LANES)
            odd = slice((2 * c + 1) * step * LANES, ((2 * c + 1) * step + 1) * LANES)
            lo_scr[rows, even] = jnp.where(lo, chunk, 0.0).astype(BF16)
            hi_scr[rows, even] = jnp.where(lo, 0.0, swapped).astype(BF16)
            lo_scr[rows, odd] = jnp.where(lo, swapped, 0.0).astype(BF16)
            hi_scr[rows, odd] = jnp.where(lo, 0.0, chunk).astype(BF16)

    def scores(j, buf):
        sel = jnp.where(i == 0, 1, 0) if j == 0 else 0
        r0 = j * BLOCK
        for kv in range(nkv):
            qp = jnp.concatenate(
                [q_scr[r0:r0 + BLOCK, (kv * grp + 2 * pr) * hd:(kv * grp + 2 * pr + 2) * hd]
                 for pr in range(grp // 2)], axis=0)
            win = (slice(r0, r0 + 2 * BLOCK), slice(kv * LANES, (kv + 1) * LANES))
            for par, k_scr in enumerate((klo_scr, khi_scr)):
                rs = slice(kv * unit + par * half, kv * unit + (par + 1) * half)
                t = _dot_nt(qp, k_scr[win]) + bias_ref[sel, rs, :]
                t_scr[buf, rs, :] = t
                m_scr[buf, rs, :] = jnp.maximum(jnp.max(t, axis=-1, keepdims=True), sink_ref[rs, :])

    def probs(j, buf):
        for c in range(nkv * unit // SM_ROWS):
            rs = slice(c * SM_ROWS, (c + 1) * SM_ROWS)
            m = m_scr[buf, rs, :]
            p_scr[buf, rs, :] = jnp.concatenate(
                [jnp.exp2(t_scr[buf, rs, q * LANES:(q + 1) * LANES] - m)
                 for q in range(2 * BLOCK // LANES)], axis=1).astype(BF16)

    def values(j, buf):
        r0 = j * BLOCK
        for kv in range(nkv):
            win = (slice(r0, r0 + 2 * BLOCK), slice(2 * kv * LANES, (2 * kv + 2) * LANES))
            ev = slice(kv * unit, kv * unit + half)
            od = slice(kv * unit + half, (kv + 1) * unit)
            ol = _dot(p_scr[buf, ev, :], vlo_scr[win]) + _dot(p_scr[buf, od, :], vhi_scr[win])
            m = jnp.where(lo, m_scr[buf, ev, :], m_scr[buf, od, :])
            denom = ol[:, LANES:] + jnp.exp2(sinkp_ref[kv * half:(kv + 1) * half, :] - m)
            o = ol[:, :LANES] * (1.0 / denom)
            for pr in range(grp // 2):
                c0 = (kv * grp + 2 * pr) * hd
                o_scr[r0:r0 + BLOCK, c0:c0 + LANES] = o[pr * BLOCK:(pr + 1) * BLOCK, :].astype(BF16)

    scores(0, 0)
    for j in range(nblk):
        if j + 1 < nblk:
            scores(j + 1, (j + 1) % 2)
        probs(j, j % 2)
        values(j, j % 2)

    y_ref[...] = x_ref[...] + _dot(o_scr[...], wo_ref[...])

    for kv in range(nkv):
        for k_scr, v_scr in ((klo_scr, vlo_scr), (khi_scr, vhi_scr)):
            k_scr[0:BLOCK, kv * LANES:(kv + 1) * LANES] = k_scr[tm:tm + BLOCK, kv * LANES:(kv + 1) * LANES]
            v_scr[0:BLOCK, 2 * kv * LANES:(2 * kv + 1) * LANES] = v_scr[tm:tm + BLOCK, 2 * kv * LANES:(2 * kv + 1) * LANES]


def _alibi_slopes(nh):
    return (2.0 ** (-8.0 * np.arange(1, nh + 1, dtype=np.float32) / nh)).astype(np.float32)


def _attn_prompt(x, g, wqkv, wo, sinks, *, tm, nh, nkv, hd):
    bsz, seq, d = x.shape
    grp = nh // nkv
    nk = nkv * hd
    assert 2 * hd == LANES and grp % 2 == 0 and tm % BLOCK == 0
    half = grp // 2 * BLOCK
    r = np.arange(grp * BLOCK)
    head_in_kv = 2 * ((r % half) // BLOCK) + r // half
    head = (np.arange(nkv)[:, None] * grp + head_in_kv[None, :]).reshape(-1)
    nrows = nkv * grp * BLOCK
    qi = np.tile(np.arange(BLOCK), nkv * grp)[:, None]
    si = np.arange(2 * BLOCK)[None, :]
    dist = qi - si + BLOCK
    band = (dist >= 0) & (dist <= BLOCK)
    slopes = _alibi_slopes(nh)[head][:, None]
    bias = np.where(band, -slopes * dist.astype(np.float32) * np.float32(LOG2E), np.float32(NEG_INF))
    bias = np.stack([bias, np.where(si < BLOCK, np.float32(NEG_INF), bias)]).astype(np.float32)
    sink2 = sinks.astype(F32) * LOG2E
    sink = jnp.broadcast_to(sink2[head][:, None], (nrows, LANES))
    even_head = np.repeat((np.arange(nkv)[:, None] * grp + 2 * np.arange(grp // 2)[None, :]).reshape(-1), BLOCK)
    sink_pair = jnp.where(np.arange(LANES)[None, :] < hd, sink2[even_head][:, None], sink2[even_head + 1][:, None])

    kern = functools.partial(_attn_prompt_kernel, tm=tm, nh=nh, nkv=nkv, hd=hd)
    return pl.pallas_call(
        kern,
        grid=(bsz, seq // tm),
        in_specs=[
            pl.BlockSpec((None, tm, d), lambda b, i: (b, i, 0)),
            _full((1, d)),
            _full(wqkv.shape),
            _full(wo.shape),
            _full(bias.shape),
            _full(sink.shape),
            _full(sink_pair.shape),
        ],
        out_specs=[
            pl.BlockSpec((None, tm, d), lambda b, i: (b, i, 0)),
            pl.BlockSpec((None, BLOCK, nk), lambda b, i: (b, 0, 0)),
            pl.BlockSpec((None, BLOCK, nk), lambda b, i: (b, 0, 0)),
        ],
        out_shape=[
            jax.ShapeDtypeStruct((bsz, seq, d), F32),
            jax.ShapeDtypeStruct((bsz, BLOCK, nk), F32),
            jax.ShapeDtypeStruct((bsz, BLOCK, nk), F32),
        ],
        scratch_shapes=[
            pltpu.VMEM((tm, nh * hd), BF16),
            pltpu.VMEM((tm + BLOCK, nkv * LANES), BF16),
            pltpu.VMEM((tm + BLOCK, nkv * LANES), BF16),
            pltpu.VMEM((tm + BLOCK, 2 * nkv * LANES), BF16),
            pltpu.VMEM((tm + BLOCK, 2 * nkv * LANES), BF16),
            pltpu.VMEM((2, nrows, 2 * BLOCK), F32),
            pltpu.VMEM((2, nrows, LANES), F32),
            pltpu.VMEM((2, nrows, 2 * BLOCK), BF16),
            pltpu.VMEM((tm, nh * hd), BF16),
        ],
        compiler_params=_params(("arbitrary", "arbitrary")),
        name="attn_prompt",
    )(x, g.reshape(1, d), wqkv, wo, bias, sink, sink_pair)


def _rms_matmul_kernel(x_ref, g_ref, w_ref, y_ref):
    h = _rms(x_ref[...], g_ref[...]).astype(BF16)
    y_ref[...] = _dot(h, w_ref[...])


def _rms_matmul(x, g, w):
    n, d = x.shape
    return pl.pallas_call(
        _rms_matmul_kernel,
        grid=(1,),
        in_specs=[_full((n, d)), _full((1, d)), _full(w.shape)],
        out_specs=_whole((n, w.shape[1])),
        out_shape=jax.ShapeDtypeStruct((n, w.shape[1]), F32),
        compiler_params=_params(("arbitrary",)),
        name="rms_matmul",
    )(x, g.reshape(1, d), w)


def _attn_sample_kernel(q_ref, kn_ref, vn_ref, ck_ref, cv_ref, e_ref, et_ref, mask_ref,
                        bias_ref, sink_ref, o_ref, nk_ref, nv_ref,
                        qx_scr, kk_scr, vv_scr, s_scr, p_scr, ox_scr,
                        *, bb, rows, t_new, hd):
    win = ck_ref.shape[1]

    @pl.when(pl.program_id(0) == 0)
    def _():
        kk_scr[:, win:, :] = jnp.zeros((bb, win, kk_scr.shape[2]), BF16)
        vv_scr[:, win:, :] = jnp.zeros((bb, win, vv_scr.shape[2]), BF16)

    q = (q_ref[...].reshape(bb * rows, hd) * (hd ** -0.5 * LOG2E)).astype(BF16)
    mask = mask_ref[...]
    qx_scr[...] = (_dot(q, e_ref[...]) * mask).astype(BF16)
    for b in range(bb):
        kk_scr[b, 0:win, :] = ck_ref[b].astype(BF16)
        kk_scr[b, win:win + t_new, :] = kn_ref[b].astype(BF16)
        vv_scr[b, 0:win, :] = cv_ref[b].astype(BF16)
        vv_scr[b, win:win + t_new, :] = vn_ref[b].astype(BF16)
        for src, new, dst in ((ck_ref, kn_ref, nk_ref), (cv_ref, vn_ref, nv_ref)):
            dst[b, 0:win - t_new, :] = src[b, t_new:win, :]
            dst[b, win - t_new:win, :] = new[b]
    for b in range(bb):
        rs = slice(b * rows, (b + 1) * rows)
        s_scr[rs, :] = _dot_nt(qx_scr[rs, :], kk_scr[b])

    t = s_scr[...] + bias_ref[...]
    sink = sink_ref[...]
    m = jnp.maximum(jnp.max(t, axis=-1, keepdims=True), sink)
    p = jnp.concatenate([jnp.exp2(t[:, c * LANES:(c + 1) * LANES] - m)
                         for c in range(2 * win // LANES)], axis=1)
    p_scr[...] = p.astype(BF16)
    inv = 1.0 / (jnp.sum(p, axis=-1, keepdims=True) + jnp.exp2(sink - m))
    scale = jnp.concatenate([inv] * (mask.shape[1] // LANES), axis=1) * mask
    for b in range(bb):
        rs = slice(b * rows, (b + 1) * rows)
        ox_scr[rs, :] = (_dot(p_scr[rs, :], vv_scr[b]) * scale[rs, :]).astype(BF16)
    o_ref[...] = _dot(ox_scr[...], et_ref[...]).reshape(bb, rows, hd)


def _attn_sample(q3, kn3, vn3, ck, cv, sinks, *, nh, nkv, hd, bb):
    dbsz, rows, _ = q3.shape
    t_new = kn3.shape[1]
    win = ck.shape[1]
    nk = nkv * hd
    grp = nh // nkv
    assert t_new <= win and win % LANES == 0 and nk % LANES == 0
    row_t = np.arange(rows)[:, None] // nh
    row_h = np.arange(rows) % nh
    row_slope = _alibi_slopes(nh)[row_h][:, None]
    key = np.arange(2 * win)[None, :]
    dist = (row_t - key + win).astype(np.float32)
    valid = (key >= row_t) & (key <= row_t + win) & (key < win + t_new)
    bias = np.where(valid, -row_slope * dist * np.float32(LOG2E), np.float32(NEG_INF)).astype(np.float32)
    bias = np.tile(bias, (bb, 1))
    mask = (np.arange(nk)[None, :] // hd == (row_h // grp)[:, None]).astype(np.float32)
    mask = np.tile(mask, (bb, 1))
    e = np.tile(np.eye(hd, dtype=np.float32), (1, nkv))
    sink = jnp.broadcast_to((sinks.astype(F32) * LOG2E)[np.tile(row_h, bb)][:, None], (bb * rows, LANES))

    kern = functools.partial(_attn_sample_kernel, bb=bb, rows=rows, t_new=t_new, hd=hd)
    return pl.pallas_call(
        kern,
        grid=(dbsz // bb,),
        in_specs=[
            pl.BlockSpec((bb, rows, hd), lambda i: (i, 0, 0)),
            pl.BlockSpec((bb, t_new, nk), lambda i: (i, 0, 0)),
            pl.BlockSpec((bb, t_new, nk), lambda i: (i, 0, 0)),
            pl.BlockSpec((bb, win, nk), lambda i: (i, 0, 0)),
            pl.BlockSpec((bb, win, nk), lambda i: (i, 0, 0)),
            _full(e.shape), _full(e.T.shape), _full(mask.shape),
            _full(bias.shape), _full(sink.shape),
        ],
        out_specs=[
            pl.BlockSpec((bb, rows, hd), lambda i: (i, 0, 0)),
            pl.BlockSpec((bb, win, nk), lambda i: (i, 0, 0)),
            pl.BlockSpec((bb, win, nk), lambda i: (i, 0, 0)),
        ],
        out_shape=[
            jax.ShapeDtypeStruct((dbsz, rows, hd), F32),
            jax.ShapeDtypeStruct((dbsz, win, nk), F32),
            jax.ShapeDtypeStruct((dbsz, win, nk), F32),
        ],
        scratch_shapes=[
            pltpu.VMEM((bb * rows, nk), BF16),
            pltpu.VMEM((bb, 2 * win, nk), BF16),
            pltpu.VMEM((bb, 2 * win, nk), BF16),
            pltpu.VMEM((bb * rows, 2 * win), F32),
            pltpu.VMEM((bb * rows, 2 * win), BF16),
            pltpu.VMEM((bb * rows, nk), BF16),
        ],
        compiler_params=_params(("arbitrary",)),
        name="attn_sample",
    )(q3, kn3, vn3, ck, cv, jnp.asarray(e, BF16), jnp.asarray(e.T, BF16), mask, bias, sink)


def _ffn_kernel(*refs, dff, chunk, proj, final):
    refs = list(refs)
    x_ref = refs.pop(0)
    if proj:
        o_ref, wo_ref = refs.pop(0), refs.pop(0)
    g_ref, win_ref, wout_ref = refs.pop(0), refs.pop(0), refs.pop(0)
    if final:
        gf_ref = refs.pop(0)
    y_ref, a_scr = refs

    res_ref = x_ref
    if proj:
        y_ref[...] = x_ref[...] + _dot(o_ref[...].astype(BF16), wo_ref[...])
        res_ref = y_ref
    h = _rms(res_ref[...], g_ref[...]).astype(BF16)
    for c in range(dff // chunk):
        gate = _dot(h, win_ref[:, c * chunk:(c + 1) * chunk])
        up = _dot(h, win_ref[:, dff + c * chunk:dff + (c + 1) * chunk])
        a_scr[:, c * chunk:(c + 1) * chunk] = (gate * jax.nn.sigmoid(gate) * up).astype(BF16)
    y = res_ref[...] + _dot(a_scr[...], wout_ref[...])
    if final:
        y = _rms(y, gf_ref[...])
    y_ref[...] = y


def _ffn(x, g, win, wout, *, tm, o=None, wo=None, gf=None, chunk=256):
    n, d = x.shape
    dff = wout.shape[0]
    proj, final = o is not None, gf is not None
    tile = pl.BlockSpec((tm, d), lambda i: (i, 0))
    args, specs = [x], [tile]
    if proj:
        args += [o, wo]
        specs += [pl.BlockSpec((tm, o.shape[1]), lambda i: (i, 0)), _full(wo.shape)]
    args += [g.reshape(1, d), win, wout]
    specs += [_full((1, d)), _full(win.shape), _full(wout.shape)]
    if final:
        args.append(gf.reshape(1, d))
        specs.append(_full((1, d)))
    kern = functools.partial(_ffn_kernel, dff=dff, chunk=chunk, proj=proj, final=final)
    return pl.pallas_call(
        kern,
        grid=(n // tm,),
        in_specs=specs,
        out_specs=tile,
        out_shape=jax.ShapeDtypeStruct((n, d), F32),
        scratch_shapes=[pltpu.VMEM((tm, dff), BF16)],
        compiler_params=_params(("arbitrary",)),
        name="ffn",
    )(*args)


def _lru_coeffs(xc, wrg_ref, brg_ref, wig_ref, big_ref, lam_ref, a_out, b_out):
    d = xc.shape[1]
    blk = d // N_LRU_BLOCKS
    for n in range(N_LRU_BLOCKS):
        sl = slice(n * blk, (n + 1) * blk)
        xcn = xc[:, sl]
        xcb = xcn.astype(BF16)
        r = jax.nn.sigmoid(_dot(xcb, wrg_ref[n]) + brg_ref[:, sl])
        ig = jax.nn.sigmoid(_dot(xcb, wig_ref[n]) + big_ref[:, sl])
        log_a = -LRU_C * r * _softplus(-lam_ref[:, sl])
        a = jnp.exp(log_a)
        a_out[:, sl] = a
        b_out[:, sl] = jnp.sqrt(-jnp.tanh(log_a) * (a * a + 1.0)) * (ig * xcn)


def _rec_prompt_kernel(x_ref, g_ref, win_ref, cw_ref, cb_ref, wrg_ref, brg_ref, wig_ref, big_ref,
                       lam_ref, wout_ref, y_ref, conv_ref, hout_ref,
                       xb_scr, tail_scr, gate_scr, a_scr, b_scr, hl_scr, pl_scr, carry_scr, *, tm):
    i = pl.program_id(1)
    last = pl.num_programs(1) - 1
    d = x_ref.shape[-1]
    ns = tm // SUBLANES
    npre = CONV_WIDTH - 1
    head = npre * SUBLANES

    @pl.when(i == 0)
    def _():
        tail_scr[...] = jnp.zeros((head, d), F32)
        carry_scr[...] = jnp.zeros((1, d), F32)

    x = x_ref[...].reshape(SUBLANES, ns, d).swapaxes(0, 1).reshape(tm, d)
    h = _rms(x, g_ref[...]).astype(BF16)
    gx = _dot(h, win_ref[...])
    gate_scr[...] = jax.nn.gelu(gx[:, :d])
    xb_scr[head:head + tm, :] = gx[:, d:]

    seg = lax.broadcasted_iota(jnp.int32, (SUBLANES, 1), 0)
    for k in range(npre):
        rows = slice(k * SUBLANES, (k + 1) * SUBLANES)
        own = pltpu.roll(xb_scr[tm + k * SUBLANES:tm + (k + 1) * SUBLANES, :], 1, axis=0)
        prev = pltpu.roll(tail_scr[rows, :], 1, axis=0)
        xb_scr[rows, :] = jnp.where(seg == 0, prev, own)
    tail_scr[...] = xb_scr[tm:tm + head, :]

    xc = cb_ref[...]
    for j in range(CONV_WIDTH):
        xc = xc + xb_scr[j * SUBLANES:j * SUBLANES + tm, :] * cw_ref[j:j + 1, :]

    _lru_coeffs(xc, wrg_ref, brg_ref, wig_ref, big_ref, lam_ref, a_scr, b_scr)

    def body(s, carry):
        hloc, prod = carry
        rows = pl.ds(pl.multiple_of(s * SUBLANES, SUBLANES), SUBLANES)
        a = a_scr[rows, :]
        hloc = a * hloc + b_scr[rows, :]
        prod = a * prod
        hl_scr[rows, :] = hloc
        pl_scr[rows, :] = prod
        return hloc, prod

    hend, pend = lax.fori_loop(0, ns, body, (jnp.zeros((SUBLANES, d), F32), jnp.ones((SUBLANES, d), F32)),
                               unroll=4)

    a = jnp.where(seg == 0, 0.0, pltpu.roll(pend, 1, axis=0))
    b = jnp.where(seg == 0, carry_scr[...], pltpu.roll(hend, 1, axis=0))
    shift = 1
    while shift < SUBLANES:
        keep = seg >= shift
        b = jnp.where(keep, a * pltpu.roll(b, shift, axis=0) + b, b)
        a = jnp.where(keep, a * pltpu.roll(a, shift, axis=0), a)
        shift *= 2
    hstart = b
    hfinal = hend + pend * hstart
    carry = hfinal[SUBLANES - 1:SUBLANES, :]
    carry_scr[...] = carry

    hs = hl_scr[...].reshape(ns, SUBLANES, d) + pl_scr[...].reshape(ns, SUBLANES, d) * hstart[None]
    y = _dot((hs.reshape(tm, d) * gate_scr[...]).astype(BF16), wout_ref[...])
    y_ref[...] = x_ref[...] + y.reshape(ns, SUBLANES, d).swapaxes(0, 1).reshape(tm, d)

    @pl.when(i == last)
    def _():
        for k in range(npre):
            r = head + (ns - npre + k) * SUBLANES + SUBLANES - 1
            conv_ref[k:k + 1, :] = xb_scr[r:r + 1, :]
        hout_ref[...] = carry


def _rec_weight_specs(d, win, wrg, wig, wout):
    return [_full((1, d)), _full(win.shape), _full((CONV_WIDTH, d)), _full((1, d)),
            _full(wrg.shape), _full((1, d)), _full(wig.shape), _full((1, d)), _full((1, d)),
            _full(wout.shape)]


def _rec_prompt(x, g, win, cw, cb, wrg, brg, wig, big, lam, wout, *, tm):
    bsz, seq, d = x.shape
    kern = functools.partial(_rec_prompt_kernel, tm=tm)
    head = (CONV_WIDTH - 1) * SUBLANES
    assert tm // SUBLANES >= CONV_WIDTH - 1
    return pl.pallas_call(
        kern,
        grid=(bsz, seq // tm),
        in_specs=[pl.BlockSpec((None, tm, d), lambda b, i: (b, i, 0))]
        + _rec_weight_specs(d, win, wrg, wig, wout),
        out_specs=[
            pl.BlockSpec((None, tm, d), lambda b, i: (b, i, 0)),
            pl.BlockSpec((None, CONV_WIDTH - 1, d), lambda b, i: (b, 0, 0)),
            pl.BlockSpec((None, 1, d), lambda b, i: (b, 0, 0)),
        ],
        out_shape=[
            jax.ShapeDtypeStruct((bsz, seq, d), F32),
            jax.ShapeDtypeStruct((bsz, CONV_WIDTH - 1, d), F32),
            jax.ShapeDtypeStruct((bsz, 1, d), F32),
        ],
        scratch_shapes=[
            pltpu.VMEM((tm + head, d), F32),
            pltpu.VMEM((head, d), F32),
            pltpu.VMEM((tm, d), F32),
            pltpu.VMEM((tm, d), F32),
            pltpu.VMEM((tm, d), F32),
            pltpu.VMEM((tm, d), F32),
            pltpu.VMEM((tm, d), F32),
            pltpu.VMEM((1, d), F32),
        ],
        compiler_params=_params(("arbitrary", "arbitrary")),
        name="rec_prompt",
    )(x, g.reshape(1, d), win, cw, cb.reshape(1, d), wrg, brg.reshape(1, d), wig,
      big.reshape(1, d), lam.reshape(1, d), wout)


def _rec_sample_kernel(x_ref, sc_ref, h0_ref, g_ref, win_ref, cw_ref, cb_ref, wrg_ref, brg_ref,
                       wig_ref, big_ref, lam_ref, wout_ref, y_ref, conv_ref, hout_ref,
                       a_scr, b_scr, hs_scr, *, t_new, nb):
    d = x_ref.shape[-1]
    x = x_ref[...]
    h = _rms(x, g_ref[...]).astype(BF16)
    gx = _dot(h, win_ref[...])
    gate = gx[:, :d]
    xb = gx[:, d:]
    xp = [sc_ref[j] for j in range(CONV_WIDTH - 1)] + [xb[t * nb:(t + 1) * nb, :] for t in range(t_new)]
    xcs = []
    for t in range(t_new):
        acc = cb_ref[...]
        for j in range(CONV_WIDTH):
            acc = acc + xp[t + j] * cw_ref[j:j + 1, :]
        xcs.append(acc)
    xc = jnp.concatenate(xcs, axis=0)

    _lru_coeffs(xc, wrg_ref, brg_ref, wig_ref, big_ref, lam_ref, a_scr, b_scr)

    hprev = h0_ref[...]
    for t in range(t_new):
        sl = slice(t * nb, (t + 1) * nb)
        hprev = a_scr[sl, :] * hprev + b_scr[sl, :]
        hs_scr[sl, :] = hprev
    y_ref[...] = x + _dot((hs_scr[...] * jax.nn.gelu(gate)).astype(BF16), wout_ref[...])
    for j in range(CONV_WIDTH - 1):
        conv_ref[j] = xp[t_new + j]
    hout_ref[...] = hprev


def _rec_sample(x, sc, h0, g, win, cw, cb, wrg, brg, wig, big, lam, wout, *, t_new):
    n, d = x.shape
    nb = n // t_new
    kern = functools.partial(_rec_sample_kernel, t_new=t_new, nb=nb)
    return pl.pallas_call(
        kern,
        grid=(1,),
        in_specs=[_full((n, d)), _full(sc.shape), _full(h0.shape)]
        + _rec_weight_specs(d, win, wrg, wig, wout),
        out_specs=[_whole((n, d)), _whole(sc.shape), _whole(h0.shape)],
        out_shape=[
            jax.ShapeDtypeStruct((n, d), F32),
            jax.ShapeDtypeStruct(sc.shape, F32),
            jax.ShapeDtypeStruct(h0.shape, F32),
        ],
        scratch_shapes=[pltpu.VMEM((n, d), F32)] * 3,
        compiler_params=_params(("arbitrary",)),
        name="rec_sample",
    )(x, sc, h0, g.reshape(1, d), win, cw, cb.reshape(1, d), wrg, brg.reshape(1, d), wig,
      big.reshape(1, d), lam.reshape(1, d), wout)


def _pick_tile(seq, want):
    tm = min(want, seq)
    while seq % tm:
        tm //= 2
    return tm


def kernel(x_prompt, x_sample, cache_k, cache_v, state_conv, state_h, attn_norm, w_qkv, w_attn_out,
           attn_sinks, rec_norm, w_rec_in, conv_w, conv_b, w_rgate, b_rgate, w_igate, b_igate,
           lru_lambda, w_rec_out, ffn_norm, w_ffn_in, w_ffn_out, final_norm):
    bsz, seq, d = x_prompt.shape
    dbsz, t_new, _ = x_sample.shape
    depth = ffn_norm.shape[0]
    nh = attn_sinks.shape[1]
    hd = w_attn_out.shape[1] // nh
    nkv = (w_qkv.shape[2] // hd - nh) // 2
    nq, nk = nh * hd, nkv * hd
    win = cache_k.shape[2]
    tm = _pick_tile(seq, TILE_TOKENS)
    tm_ffn = _pick_tile(bsz * seq, FFN_TILE_TOKENS)
    tm_s = dbsz * t_new
    bb = _pick_tile(dbsz, 8)

    xp = x_prompt
    xs = x_sample.reshape(dbsz * t_new, d)
    time_major = False
    nk_p, nv_p, nk_s, nv_s, nc_p, nh_p, nc_s, nh_s = [], [], [], [], [], [], [], []

    for layer in range(depth):
        j = layer // 2
        wfi = w_ffn_in[layer].astype(BF16)
        wfo = w_ffn_out[layer].astype(BF16)
        gf = final_norm if layer == depth - 1 else None
        if layer % 2 == 0:
            wqkv = w_qkv[j].astype(BF16)
            wo = w_attn_out[j].astype(BF16)
            xp, kp, vp = _attn_prompt(xp, attn_norm[j], wqkv, wo, attn_sinks[j],
                                      tm=tm, nh=nh, nkv=nkv, hd=hd)
            nk_p.append(kp.reshape(bsz, BLOCK, nkv, hd))
            nv_p.append(vp.reshape(bsz, BLOCK, nkv, hd))

            if time_major:
                xs = xs.reshape(t_new, dbsz, d).transpose(1, 0, 2).reshape(dbsz * t_new, d)
                time_major = False
            qkv = _rms_matmul(xs, attn_norm[j], wqkv)
            q3 = qkv[:, :nq].reshape(dbsz, t_new * nh, hd)
            kn3 = qkv[:, nq:nq + nk].reshape(dbsz, t_new, nk)
            vn3 = qkv[:, nq + nk:].reshape(dbsz, t_new, nk)
            ck = cache_k[j].reshape(dbsz, win, nk)
            cv = cache_v[j].reshape(dbsz, win, nk)
            o3, nks, nvs = _attn_sample(q3, kn3, vn3, ck, cv, attn_sinks[j], nh=nh, nkv=nkv, hd=hd, bb=bb)
            nk_s.append(nks.reshape(dbsz, win, nkv, hd))
            nv_s.append(nvs.reshape(dbsz, win, nkv, hd))
            xs = _ffn(xs, ffn_norm[layer], wfi, wfo, tm=tm_s, o=o3.reshape(dbsz * t_new, nq), wo=wo, gf=gf)
        else:
            rec_w = (rec_norm[j], w_rec_in[j].astype(BF16), conv_w[j], conv_b[j],
                     w_rgate[j].astype(BF16), b_rgate[j], w_igate[j].astype(BF16), b_igate[j],
                     lru_lambda[j], w_rec_out[j].astype(BF16))
            xp, cp, hp = _rec_prompt(xp, *rec_w, tm=tm)
            nc_p.append(cp)
            nh_p.append(hp.reshape(bsz, d))

            if not time_major:
                xs = xs.reshape(dbsz, t_new, d).transpose(1, 0, 2).reshape(t_new * dbsz, d)
                time_major = True
            sc = state_conv[j].transpose(1, 0, 2)
            xs, cs, hs = _rec_sample(xs, sc, state_h[j], *rec_w, t_new=t_new)
            nc_s.append(cs.transpose(1, 0, 2))
            nh_s.append(hs)
            xs = _ffn(xs, ffn_norm[layer], wfi, wfo, tm=tm_s, gf=gf)
        xp = _ffn(xp.reshape(bsz * seq, d), ffn_norm[layer], wfi, wfo, tm=tm_ffn, gf=gf).reshape(bsz, seq, d)

    if time_major:
        ys = xs.reshape(t_new, dbsz, d).transpose(1, 0, 2)
    else:
        ys = xs.reshape(dbsz, t_new, d)
    return (xp, ys, jnp.stack(nk_p), jnp.stack(nv_p), jnp.stack(nk_s), jnp.stack(nv_s),
            jnp.stack(nc_p), jnp.stack(nh_p), jnp.stack(nc_s), jnp.stack(nh_s))
```

```python
import functools

import numpy as np

import jax
import jax.numpy as jnp
from jax import lax
from jax.experimental import pallas as pl
from jax.experimental.pallas import tpu as pltpu

EPS = 1e-6
NEG_INF = -1e30
LRU_C = 8.0
CONV_WIDTH = 4
N_LRU_BLOCKS = 4
BLOCK = 128
SUBLANES = 8
LANES = 128
SM_ROWS = 64
TILE_TOKENS = 1024
FFN_TILE_TOKENS = 1024
ATTN_SUB_TOKENS = 512
REC_SUB_TOKENS = 512
LOG2E = 1.4426950408889634
VMEM_LIMIT = 56 * 1024 * 1024

F32 = jnp.float32
BF16 = jnp.bfloat16


def _rms(x, g):
    return x * lax.rsqrt(jnp.mean(x * x, axis=-1, keepdims=True) + EPS) * g


def _dot(a, b):
    return jnp.dot(a, b, preferred_element_type=F32)


def _dot_nt(a, b):
    return lax.dot_general(a, b, (((1,), (1,)), ((), ())), preferred_element_type=F32)


def _softplus(z):
    return jnp.maximum(z, 0.0) + jnp.log1p(jnp.exp(-jnp.abs(z)))


def _full(shape):
    nd = len(shape)
    return pl.BlockSpec(shape, lambda *_: (0,) * nd, pipeline_mode=pl.Buffered(1))


def _whole(shape):
    nd = len(shape)
    return pl.BlockSpec(shape, lambda *_: (0,) * nd)


def _params(sem, flags=None):
    return pltpu.CompilerParams(dimension_semantics=sem, vmem_limit_bytes=VMEM_LIMIT, flags=flags)


def _attn_prompt_kernel(x_ref, g_ref, wqkv_ref, wo_ref, bias_ref, sink_ref, sinkp_ref,
                        y_ref, kout_ref, vout_ref,
                        h_scr, q_scr, klo_scr, khi_scr, vlo_scr, vhi_scr, t_scr, m_scr, p_scr, o_scr,
                        *, tm, sub, nh, nkv, hd):
    i = pl.program_id(1)
    grp = nh // nkv
    nq, nk = nh * hd, nkv * hd
    half = (grp // 2) * BLOCK
    unit = 2 * half
    nsub = tm // sub
    lo = lax.broadcasted_iota(jnp.int32, (1, LANES), 1) < hd

    @pl.when(i == 0)
    def _():
        ones_lo = jnp.broadcast_to(jnp.where(lo, 1.0, 0.0).astype(BF16), (tm + BLOCK, LANES))
        ones_hi = jnp.broadcast_to(jnp.where(lo, 0.0, 1.0).astype(BF16), (tm + BLOCK, LANES))
        for kv in range(nkv):
            klo_scr[0:BLOCK, kv * LANES:(kv + 1) * LANES] = jnp.zeros((BLOCK, LANES), BF16)
            khi_scr[0:BLOCK, kv * LANES:(kv + 1) * LANES] = jnp.zeros((BLOCK, LANES), BF16)
            vlo_scr[0:BLOCK, 2 * kv * LANES:(2 * kv + 1) * LANES] = jnp.zeros((BLOCK, LANES), BF16)
            vhi_scr[0:BLOCK, 2 * kv * LANES:(2 * kv + 1) * LANES] = jnp.zeros((BLOCK, LANES), BF16)
            vlo_scr[:, (2 * kv + 1) * LANES:(2 * kv + 2) * LANES] = ones_lo
            vhi_scr[:, (2 * kv + 1) * LANES:(2 * kv + 2) * LANES] = ones_hi

    def qkv_ops(t):
        tok = slice(t * sub, (t + 1) * sub)
        rows = slice(BLOCK + t * sub, BLOCK + (t + 1) * sub)

        def norm():
            h_scr[tok, :] = _rms(x_ref[tok, :], g_ref[...]).astype(BF16)

        def q_op(c):
            cols = slice(c * nk, (c + 1) * nk)
            q_scr[tok, cols] = (_dot(h_scr[tok, :], wqkv_ref[:, cols]) * (hd ** -0.5 * LOG2E)).astype(BF16)

        def kv_op(which):
            lo_scr, hi_scr, step, out_ref = ((klo_scr, khi_scr, 1, kout_ref), (vlo_scr, vhi_scr, 2, vout_ref))[which]
            val = _dot(h_scr[tok, :], wqkv_ref[:, nq + which * nk:nq + (which + 1) * nk])
            if t == nsub - 1:
                out_ref[...] = val[sub - BLOCK:, :]
            for c in range(nk // LANES):
                chunk = val[:, c * LANES:(c + 1) * LANES]
                swapped = pltpu.roll(chunk, hd, axis=1)
                even = slice(2 * c * step * LANES, (2 * c * step + 1) * LANES)
                odd = slice((2 * c + 1) * step * LANES, ((2 * c + 1) * step + 1) * LANES)
                lo_scr[rows, even] = jnp.where(lo, chunk, 0.0).astype(BF16)
                hi_scr[rows, even] = jnp.where(lo, 0.0, swapped).astype(BF16)
                lo_scr[rows, odd] = jnp.where(lo, swapped, 0.0).astype(BF16)
                hi_scr[rows, odd] = jnp.where(lo, 0.0, chunk).astype(BF16)

        return ([norm] + [functools.partial(q_op, c) for c in range(nq // nk)]
                + [functools.partial(kv_op, which) for which in range(2)])

    def scores(j, buf):
        sel = jnp.where(i == 0, 1, 0) if j == 0 else 0
        r0 = j * BLOCK
        for kv in range(nkv):
            qp = jnp.concatenate(
                [q_scr[r0:r0 + BLOCK, (kv * grp + 2 * pr) * hd:(kv * grp + 2 * pr + 2) * hd]
                 for pr in range(grp // 2)], axis=0)
            win = (slice(r0, r0 + 2 * BLOCK), slice(kv * LANES, (kv + 1) * LANES))
            for par, k_scr in enumerate((klo_scr, khi_scr)):
                rs = slice(kv * unit + par * half, kv * unit + (par + 1) * half)
                t = _dot_nt(qp, k_scr[win]) + bias_ref[sel, rs, :]
                t_scr[buf, rs, :] = t
                m_scr[buf, rs, :] = jnp.maximum(jnp.max(t, axis=-1, keepdims=True), sink_ref[rs, :])

    def probs(j, buf):
        for c in range(nkv * unit // SM_ROWS):
            rs = slice(c * SM_ROWS, (c + 1) * SM_ROWS)
            m = m_scr[buf, rs, :]
            p_scr[buf, rs, :] = jnp.concatenate(
                [jnp.exp2(t_scr[buf, rs, q * LANES:(q + 1) * LANES] - m)
                 for q in range(2 * BLOCK // LANES)], axis=1).astype(BF16)

    def values(j, buf):
        r0 = j * BLOCK
        for kv in range(nkv):
            win = (slice(r0, r0 + 2 * BLOCK), slice(2 * kv * LANES, (2 * kv + 2) * LANES))
            ev = slice(kv * unit, kv * unit + half)
            od = slice(kv * unit + half, (kv + 1) * unit)
            ol = _dot(p_scr[buf, ev, :], vlo_scr[win]) + _dot(p_scr[buf, od, :], vhi_scr[win])
            m = jnp.where(lo, m_scr[buf, ev, :], m_scr[buf, od, :])
            denom = ol[:, LANES:] + jnp.exp2(sinkp_ref[kv * half:(kv + 1) * half, :] - m)
            o = ol[:, :LANES] * (1.0 / denom)
            for pr in range(grp // 2):
                c0 = (kv * grp + 2 * pr) * hd
                o_scr[r0:r0 + BLOCK, c0:c0 + LANES] = o[pr * BLOCK:(pr + 1) * BLOCK, :].astype(BF16)

    def attn_ops(t):
        blocks = range(t * sub // BLOCK, (t + 1) * sub // BLOCK)
        ops = [functools.partial(scores, blocks[0], blocks[0] % 2)]
        for j in blocks:
            if j + 1 in blocks:
                ops.append(functools.partial(scores, j + 1, (j + 1) % 2))
            ops += [functools.partial(probs, j, j % 2), functools.partial(values, j, j % 2)]
        return ops

    def out_ops(t):
        tok = slice(t * sub, (t + 1) * sub)

        def op(c):
            cols = slice(c * nk, (c + 1) * nk)
            y_ref[tok, cols] = x_ref[tok, cols] + _dot(o_scr[tok, :], wo_ref[:, cols])
        return [functools.partial(op, c) for c in range(y_ref.shape[1] // nk)]

    stages = (qkv_ops, attn_ops, out_ops)
    for phase in range(nsub + len(stages) - 1):
        active = [stages[phase - t](t) for t in range(nsub) if 0 <= phase - t < len(stages)]
        for op in _interleave(*active):
            op()

    for kv in range(nkv):
        for k_scr, v_scr in ((klo_scr, vlo_scr), (khi_scr, vhi_scr)):
            k_scr[0:BLOCK, kv * LANES:(kv + 1) * LANES] = k_scr[tm:tm + BLOCK, kv * LANES:(kv + 1) * LANES]
            v_scr[0:BLOCK, 2 * kv * LANES:(2 * kv + 1) * LANES] = v_scr[tm:tm + BLOCK, 2 * kv * LANES:(2 * kv + 1) * LANES]


def _alibi_slopes(nh):
    return (2.0 ** (-8.0 * np.arange(1, nh + 1, dtype=np.float32) / nh)).astype(np.float32)


def _attn_prompt(x, g, wqkv, wo, sinks, *, tm, nh, nkv, hd):
    bsz, seq, d = x.shape
    grp = nh // nkv
    nk = nkv * hd
    assert 2 * hd == LANES and grp % 2 == 0 and tm % BLOCK == 0
    half = grp // 2 * BLOCK
    r = np.arange(grp * BLOCK)
    head_in_kv = 2 * ((r % half) // BLOCK) + r // half
    head = (np.arange(nkv)[:, None] * grp + head_in_kv[None, :]).reshape(-1)
    nrows = nkv * grp * BLOCK
    qi = np.tile(np.arange(BLOCK), nkv * grp)[:, None]
    si = np.arange(2 * BLOCK)[None, :]
    dist = qi - si + BLOCK
    band = (dist >= 0) & (dist <= BLOCK)
    slopes = _alibi_slopes(nh)[head][:, None]
    bias = np.where(band, -slopes * dist.astype(np.float32) * np.float32(LOG2E), np.float32(NEG_INF))
    bias = np.stack([bias, np.where(si < BLOCK, np.float32(NEG_INF), bias)]).astype(np.float32)
    sink2 = (sinks.astype(F32) * LOG2E).reshape(nkv, grp // 2, 2)
    sink = jnp.broadcast_to(sink2.transpose(0, 2, 1)[:, :, :, None, None],
                            (nkv, 2, grp // 2, BLOCK, LANES)).reshape(nrows, LANES)
    sink_pair = jnp.broadcast_to(jnp.repeat(sink2, hd, axis=2)[:, :, None, :],
                                 (nkv, grp // 2, BLOCK, LANES)).reshape(nkv * half, LANES)

    sub = _pick_tile(tm, ATTN_SUB_TOKENS)
    assert sub % BLOCK == 0 and (nh * hd) % nk == 0 and d % nk == 0
    kern = functools.partial(_attn_prompt_kernel, tm=tm, sub=sub, nh=nh, nkv=nkv, hd=hd)
    return pl.pallas_call(
        kern,
        grid=(bsz, seq // tm),
        in_specs=[
            pl.BlockSpec((None, tm, d), lambda b, i: (b, i, 0)),
            _full((1, d)),
            _full(wqkv.shape),
            _full(wo.shape),
            _full(bias.shape),
            _full(sink.shape),
            _full(sink_pair.shape),
        ],
        out_specs=[
            pl.BlockSpec((None, tm, d), lambda b, i: (b, i, 0)),
            pl.BlockSpec((None, BLOCK, nk), lambda b, i: (b, 0, 0)),
            pl.BlockSpec((None, BLOCK, nk), lambda b, i: (b, 0, 0)),
        ],
        out_shape=[
            jax.ShapeDtypeStruct((bsz, seq, d), F32),
            jax.ShapeDtypeStruct((bsz, BLOCK, nk), F32),
            jax.ShapeDtypeStruct((bsz, BLOCK, nk), F32),
        ],
        scratch_shapes=[
            pltpu.VMEM((tm, d), BF16),
            pltpu.VMEM((tm, nh * hd), BF16),
            pltpu.VMEM((tm + BLOCK, nkv * LANES), BF16),
            pltpu.VMEM((tm + BLOCK, nkv * LANES), BF16),
            pltpu.VMEM((tm + BLOCK, 2 * nkv * LANES), BF16),
            pltpu.VMEM((tm + BLOCK, 2 * nkv * LANES), BF16),
            pltpu.VMEM((2, nrows, 2 * BLOCK), F32),
            pltpu.VMEM((2, nrows, LANES), F32),
            pltpu.VMEM((2, nrows, 2 * BLOCK), BF16),
            pltpu.VMEM((tm, nh * hd), BF16),
        ],
        compiler_params=_params(("arbitrary", "arbitrary")),
        name="attn_prompt",
    )(x, g.reshape(1, d), wqkv, wo, bias, sink, sink_pair)


def _rms_matmul_kernel(x_ref, g_ref, w_ref, wt_ref, y_ref, yt_ref):
    h = _rms(x_ref[...], g_ref[...]).astype(BF16)
    y_ref[...] = _dot(h, w_ref[...])
    yt_ref[...] = _dot_nt(wt_ref[...], h)


def _rms_matmul(x, g, w, wt):
    n, d = x.shape
    return pl.pallas_call(
        _rms_matmul_kernel,
        grid=(1,),
        in_specs=[_full((n, d)), _full((1, d)), _full(w.shape), _full(wt.shape)],
        out_specs=[_whole((n, w.shape[1])), _whole((wt.shape[0], n))],
        out_shape=[jax.ShapeDtypeStruct((n, w.shape[1]), F32), jax.ShapeDtypeStruct((wt.shape[0], n), F32)],
        compiler_params=_params(("arbitrary",)),
        name="rms_matmul",
    )(x, g.reshape(1, d), w, wt)


def _attn_sample_kernel(q_ref, kvt_ref, ck_ref, cv_ref, e_ref, et_ref, mask_ref,
                        bias_ref, sink_ref, o_ref, nk_ref, nv_ref,
                        qx_scr, kk_scr, vv_scr, s_scr, p_scr, ox_scr,
                        *, bb, rows, t_new, hd):
    nk, win = ck_ref.shape[1], ck_ref.shape[2]
    lane = lax.broadcasted_iota(jnp.int32, (1, win), 1)
    first_lane = (pl.program_id(0) * (bb * t_new)) % LANES

    q = (q_ref[...].reshape(bb * rows, hd) * (hd ** -0.5 * LOG2E)).astype(BF16)
    mask = mask_ref[...]
    qx_scr[...] = (_dot(q, e_ref[...]) * mask).astype(BF16)
    for b in range(bb):
        new = pltpu.roll(kvt_ref[...], (2 * LANES - t_new - first_lane - b * t_new) % LANES, axis=1)
        new = jnp.where(lane >= win - t_new, new, 0.0)
        for c_ref, scr, out_ref, part in ((ck_ref, kk_scr, nk_ref, new[:nk]), (cv_ref, vv_scr, nv_ref, new[nk:])):
            old = c_ref[b]
            scr[b, :, 0:win] = old.astype(BF16)
            scr[b, :, win:2 * win] = part.astype(BF16)
            out_ref[b] = jnp.where(lane < win - t_new, pltpu.roll(old, win - t_new, axis=1), part)
    for b in range(bb):
        rs = slice(b * rows, (b + 1) * rows)
        s_scr[rs, :] = _dot(qx_scr[rs, :], kk_scr[b])

    t = s_scr[...] + bias_ref[...]
    sink = sink_ref[...]
    m = jnp.maximum(jnp.max(t, axis=-1, keepdims=True), sink)
    p = jnp.concatenate([jnp.exp2(t[:, c * LANES:(c + 1) * LANES] - m)
                         for c in range(2 * win // LANES)], axis=1)
    p_scr[...] = p.astype(BF16)
    inv = 1.0 / (jnp.sum(p, axis=-1, keepdims=True) + jnp.exp2(sink - m))
    scale = jnp.concatenate([inv] * (mask.shape[1] // LANES), axis=1) * mask
    for b in range(bb):
        rs = slice(b * rows, (b + 1) * rows)
        ox_scr[rs, :] = (_dot_nt(p_scr[rs, :], vv_scr[b]) * scale[rs, :]).astype(BF16)
    o_ref[...] = _dot(ox_scr[...], et_ref[...]).reshape(bb, rows, hd)


def _attn_sample(q3, kvt, ck, cv, sinks, *, t_new, nh, nkv, hd, bb):
    dbsz, rows, _ = q3.shape
    nk, win = ck.shape[1], ck.shape[2]
    grp = nh // nkv
    assert win == LANES and nk == nkv * hd and LANES % (bb * t_new) == 0 and kvt.shape[1] % LANES == 0
    row_t = np.arange(rows)[:, None] // nh
    row_h = np.arange(rows) % nh
    row_slope = _alibi_slopes(nh)[row_h][:, None]
    col = np.arange(2 * win)[None, :]
    key = np.where(col < win, col, col - (win - t_new))
    dist = (row_t - key + win).astype(np.float32)
    valid = ((col < win) | (col >= 2 * win - t_new)) & (key >= row_t) & (key <= row_t + win)
    bias = np.where(valid, -row_slope * dist * np.float32(LOG2E), np.float32(NEG_INF)).astype(np.float32)
    bias = np.tile(bias, (bb, 1))
    mask = (np.arange(nk)[None, :] // hd == (row_h // grp)[:, None]).astype(np.float32)
    mask = np.tile(mask, (bb, 1))
    e = np.tile(np.eye(hd, dtype=np.float32), (1, nkv))
    sink = jnp.broadcast_to(jnp.tile(sinks.astype(F32) * LOG2E, bb * t_new)[:, None], (bb * rows, LANES))

    kern = functools.partial(_attn_sample_kernel, bb=bb, rows=rows, t_new=t_new, hd=hd)
    return pl.pallas_call(
        kern,
        grid=(dbsz // bb,),
        in_specs=[
            pl.BlockSpec((bb, rows, hd), lambda i: (i, 0, 0)),
            pl.BlockSpec((2 * nk, LANES), lambda i: (0, (i * bb * t_new) // LANES)),
            pl.BlockSpec((bb, nk, win), lambda i: (i, 0, 0)),
            pl.BlockSpec((bb, nk, win), lambda i: (i, 0, 0)),
            _full(e.shape), _full(e.T.shape), _full(mask.shape),
            _full(bias.shape), _full(sink.shape),
        ],
        out_specs=[
            pl.BlockSpec((bb, rows, hd), lambda i: (i, 0, 0)),
            pl.BlockSpec((bb, nk, win), lambda i: (i, 0, 0)),
            pl.BlockSpec((bb, nk, win), lambda i: (i, 0, 0)),
        ],
        out_shape=[
            jax.ShapeDtypeStruct((dbsz, rows, hd), F32),
            jax.ShapeDtypeStruct((dbsz, nk, win), F32),
            jax.ShapeDtypeStruct((dbsz, nk, win), F32),
        ],
        scratch_shapes=[
            pltpu.VMEM((bb * rows, nk), BF16),
            pltpu.VMEM((bb, nk, 2 * win), BF16),
            pltpu.VMEM((bb, nk, 2 * win), BF16),
            pltpu.VMEM((bb * rows, 2 * win), F32),
            pltpu.VMEM((bb * rows, 2 * win), BF16),
            pltpu.VMEM((bb * rows, nk), BF16),
        ],
        compiler_params=_params(("arbitrary",)),
        name="attn_sample",
    )(q3, kvt, ck, cv, jnp.asarray(e, BF16), jnp.asarray(e.T, BF16), mask, bias, sink)


def _time_permute(v, sub, inverse):
    n, d = v.shape
    ns = sub // SUBLANES
    shape = (ns, SUBLANES, d) if inverse else (SUBLANES, ns, d)
    parts = [v[t * sub:(t + 1) * sub, :].reshape(shape).swapaxes(0, 1).reshape(sub, d)
             for t in range(n // sub)]
    return parts[0] if len(parts) == 1 else jnp.concatenate(parts, axis=0)


def _ffn_tile(x_ref, y_ref, o_ref, wo_ref, g_ref, win_ref, wout_ref, gf_ref, a_scr, *, dff, chunk,
              perm_sub=None, perm_in=False, perm_out=False):
    proj, final = o_ref is not None, gf_ref is not None
    res_ref = x_ref
    if proj:
        y_ref[...] = x_ref[...] + _dot(o_ref[...].astype(BF16), wo_ref[...])
        res_ref = y_ref
    h = _rms(res_ref[...], g_ref[...]).astype(BF16)
    for c in range(dff // chunk):
        gate = _dot(h, win_ref[:, c * chunk:(c + 1) * chunk])
        up = _dot(h, win_ref[:, dff + c * chunk:dff + (c + 1) * chunk])
        a_scr[:, c * chunk:(c + 1) * chunk] = (gate * jax.nn.sigmoid(gate) * up).astype(BF16)
    d = y_ref.shape[1]
    sumsq = None
    for c in range(d // chunk):
        cols = slice(c * chunk, (c + 1) * chunk)
        y = res_ref[:, cols] + _dot(a_scr[...], wout_ref[:, cols])
        if perm_in or perm_out:
            y = _time_permute(y, perm_sub, inverse=perm_in)
        if final:
            part = jnp.sum(y * y, axis=-1, keepdims=True)
            sumsq = part if sumsq is None else sumsq + part
        y_ref[:, cols] = y
    if final:
        y_ref[...] = y_ref[...] * lax.rsqrt(sumsq * (1.0 / d) + EPS) * gf_ref[...]


def _ffn_kernel(*refs, nt, proj, final, **tile_kw):
    refs = list(refs)
    x_ref, xs_ref = refs.pop(0), refs.pop(0)
    os_ref, wo_ref = (refs.pop(0), refs.pop(0)) if proj else (None, None)
    g_ref, win_ref, wout_ref = refs.pop(0), refs.pop(0), refs.pop(0)
    gf_ref = refs.pop(0) if final else None
    y_ref, ys_ref, a_scr = refs
    step = pl.program_id(0)

    @pl.when(step < nt)
    def _():
        _ffn_tile(x_ref, y_ref, None, None, g_ref, win_ref, wout_ref, gf_ref, a_scr, **tile_kw)

    @pl.when(step == nt)
    def _():
        _ffn_tile(xs_ref, ys_ref, os_ref, wo_ref, g_ref, win_ref, wout_ref, gf_ref,
                  a_scr.at[pl.ds(0, xs_ref.shape[0])], dff=tile_kw["dff"], chunk=tile_kw["chunk"])


def _layer_of(stacked, layer):
    nd = stacked.ndim - 1
    return pl.BlockSpec((None,) + stacked.shape[1:], lambda *_: (layer,) + (0,) * nd,
                        pipeline_mode=pl.Buffered(1))


def _ffn(x, xs, g, win_all, wout_all, layer, *, tm, os=None, wo=None, gf=None, chunk=256, perm_sub=None,
         perm_in=False, perm_out=False):
    n, d = x.shape
    ns = xs.shape[0]
    nt = n // tm
    dff = wout_all.shape[1]
    proj, final = os is not None, gf is not None
    assert not (perm_in and perm_out) and (perm_sub is None or tm % perm_sub == 0) and ns <= tm
    tile = pl.BlockSpec((tm, d), lambda i: (jnp.minimum(i, nt - 1), 0))
    args, specs = [x, xs], [tile, _full(xs.shape)]
    if proj:
        args += [os, wo]
        specs += [_full(os.shape), _full(wo.shape)]
    args += [g.reshape(1, d), win_all, wout_all]
    specs += [_full((1, d)), _layer_of(win_all, layer), _layer_of(wout_all, layer)]
    if final:
        args.append(gf.reshape(1, d))
        specs.append(_full((1, d)))
    kern = functools.partial(_ffn_kernel, nt=nt, proj=proj, final=final, dff=dff, chunk=chunk,
                             perm_sub=perm_sub, perm_in=perm_in, perm_out=perm_out)
    return pl.pallas_call(
        kern,
        grid=(nt + 1,),
        in_specs=specs,
        out_specs=[tile, _whole(xs.shape)],
        out_shape=[jax.ShapeDtypeStruct((n, d), F32), jax.ShapeDtypeStruct(xs.shape, F32)],
        scratch_shapes=[pltpu.VMEM((tm, dff), BF16)],
        compiler_params=_params(("arbitrary",)),
        name="ffn",
    )(*args)


def _lru_block_coeffs(xcn, n, sl, wrg_ref, brg_ref, wig_ref, big_ref, lam_ref):
    xcb = xcn.astype(BF16)
    r = jax.nn.sigmoid(_dot(xcb, wrg_ref[n]) + brg_ref[:, sl])
    ig = jax.nn.sigmoid(_dot(xcb, wig_ref[n]) + big_ref[:, sl])
    log_a = r * (-LRU_C * _softplus(-lam_ref[:, sl]))
    a = jnp.exp(log_a)
    gain2 = -jnp.tanh(log_a) * (a * a + 1.0)
    gain = jnp.where(gain2 > 0.0, gain2 * lax.rsqrt(gain2), 0.0)
    return a, gain * (ig * xcn)


def _lru_coeffs(xc, wrg_ref, brg_ref, wig_ref, big_ref, lam_ref, a_out, b_out):
    blk = xc.shape[1] // N_LRU_BLOCKS
    for n in range(N_LRU_BLOCKS):
        sl = slice(n * blk, (n + 1) * blk)
        a_out[:, sl], b_out[:, sl] = _lru_block_coeffs(xc[:, sl], n, sl, wrg_ref, brg_ref, wig_ref,
                                                       big_ref, lam_ref)


def _interleave(*op_lists):
    keyed = [((k + 0.5) / len(ops), which, k, op)
             for which, ops in enumerate(op_lists) for k, op in enumerate(ops)]
    return [op for _, _, _, op in sorted(keyed, key=lambda e: e[:3])]


def _gelu_tanh(x):
    c0 = 0.7978845608028654
    inner = x * (c0 + (c0 * 0.044715) * (x * x))
    hx = 0.5 * x
    return hx + hx * jnp.tanh(inner)


def _rec_prompt_kernel(x_ref, g_ref, win_ref, cw_ref, cb_ref, wrg_ref, brg_ref, wig_ref, big_ref,
                       lam_ref, wout_ref, y_ref, conv_ref, hout_ref,
                       h_scr, xb_scr, tail_scr, gate_scr, a_scr, b_scr, hl_scr, pl_scr, hend_scr, pend_scr,
                       hg_scr, carry_scr, *, tm, sub):
    i = pl.program_id(1)
    last = pl.num_programs(1) - 1
    d = x_ref.shape[-1]
    nsub = tm // sub
    ns = sub // SUBLANES
    npre = CONV_WIDTH - 1
    head = npre * SUBLANES
    seg = lax.broadcasted_iota(jnp.int32, (SUBLANES, 1), 0)

    @pl.when(i == 0)
    def _():
        tail_scr[...] = jnp.zeros((head, d), F32)
        carry_scr[...] = jnp.zeros((1, d), F32)

    blk = d // N_LRU_BLOCKS

    def load_ops(t):
        def op():
            h_scr[t] = _rms(x_ref[t * sub:(t + 1) * sub, :], g_ref[...]).astype(BF16)
        return [op]

    def project_ops(t):
        def op(c):
            cols = slice(c * blk, (c + 1) * blk)
            part = _dot(h_scr[t], win_ref[:, cols])
            if c * blk < d:
                gate_scr[t, :, cols] = _gelu_tanh(part)
            else:
                xb_scr[t, head:head + sub, c * blk - d:(c + 1) * blk - d] = part
        return [functools.partial(op, c) for c in range(2 * d // blk)]

    def coeffs_ops(t):
        def heads():
            for k in range(npre):
                rows = slice(k * SUBLANES, (k + 1) * SUBLANES)
                tail = slice(sub + k * SUBLANES, sub + (k + 1) * SUBLANES)
                before = tail_scr[rows, :] if t == 0 else xb_scr[t - 1, tail, :]
                xb_scr[t, rows, :] = jnp.where(seg == 0, pltpu.roll(before, 1, axis=0),
                                               pltpu.roll(xb_scr[t, tail, :], 1, axis=0))

        def op(n):
            sl = slice(n * blk, (n + 1) * blk)
            xc = cb_ref[:, sl]
            for j in range(CONV_WIDTH):
                xc = xc + xb_scr[t, j * SUBLANES:j * SUBLANES + sub, sl] * cw_ref[j:j + 1, sl]
            a_scr[t, :, sl], b_scr[t, :, sl] = _lru_block_coeffs(xc, n, sl, wrg_ref, brg_ref, wig_ref,
                                                                 big_ref, lam_ref)
            hloc = b_scr[t, 0:SUBLANES, sl]
            prod = a_scr[t, 0:SUBLANES, sl]
            hl_scr[t, 0:SUBLANES, sl] = hloc
            pl_scr[t, 0:SUBLANES, sl] = prod
            for s in range(1, ns):
                rows = slice(s * SUBLANES, (s + 1) * SUBLANES)
                a = a_scr[t, rows, sl]
                hloc = a * hloc + b_scr[t, rows, sl]
                prod = a * prod
                hl_scr[t, rows, sl] = hloc
                pl_scr[t, rows, sl] = prod
            hend_scr[t, :, sl] = hloc
            pend_scr[t, :, sl] = prod
        return [heads] + [functools.partial(op, n) for n in range(N_LRU_BLOCKS)]

    def finish_ops(t):
        rows = slice(t * sub, (t + 1) * sub)

        def fix():
            hend, pend = hend_scr[t], pend_scr[t]
            a = jnp.where(seg == 0, 0.0, pltpu.roll(pend, 1, axis=0))
            b = jnp.where(seg == 0, carry_scr[...], pltpu.roll(hend, 1, axis=0))
            shift = 1
            while shift < SUBLANES:
                keep = seg >= shift
                b = jnp.where(keep, a * pltpu.roll(b, shift, axis=0) + b, b)
                a = jnp.where(keep, a * pltpu.roll(a, shift, axis=0), a)
                shift *= 2
            hstart = b
            carry_scr[...] = (hend + pend * hstart)[SUBLANES - 1:SUBLANES, :]
            hs = hl_scr[t].reshape(ns, SUBLANES, d) + pl_scr[t].reshape(ns, SUBLANES, d) * hstart[None]
            hg_scr[t] = (hs.reshape(sub, d) * gate_scr[t]).astype(BF16)

        def op(c):
            cols = slice(c * blk, (c + 1) * blk)
            y_ref[rows, cols] = x_ref[rows, cols] + _dot(hg_scr[t], wout_ref[:, cols])
        return [fix] + [functools.partial(op, c) for c in range(d // blk)]

    stages = (load_ops, project_ops, coeffs_ops, finish_ops)
    for phase in range(nsub + len(stages) - 1):
        active = [stages[phase - t](t) for t in range(nsub) if 0 <= phase - t < len(stages)]
        for op in _interleave(*active):
            op()

    @pl.when(i == last)
    def _():
        for k in range(npre):
            r = head + (ns - npre + k) * SUBLANES + SUBLANES - 1
            conv_ref[k:k + 1, :] = xb_scr[nsub - 1, r:r + 1, :]
        hout_ref[...] = carry_scr[...]

    tail_scr[...] = xb_scr[nsub - 1, sub:sub + head, :]


def _rec_weight_specs(d, win, wrg, wig, wout):
    return [_full((1, d)), _full(win.shape), _full((CONV_WIDTH, d)), _full((1, d)),
            _full(wrg.shape), _full((1, d)), _full(wig.shape), _full((1, d)), _full((1, d)),
            _full(wout.shape)]


def _rec_prompt(x, g, win, cw, cb, wrg, brg, wig, big, lam, wout, *, tm, sub):
    bsz, seq, d = x.shape
    nsub = tm // sub
    kern = functools.partial(_rec_prompt_kernel, tm=tm, sub=sub)
    head = (CONV_WIDTH - 1) * SUBLANES
    assert sub // SUBLANES >= CONV_WIDTH - 1
    return pl.pallas_call(
        kern,
        grid=(bsz, seq // tm),
        in_specs=[pl.BlockSpec((None, tm, d), lambda b, i: (b, i, 0))]
        + _rec_weight_specs(d, win, wrg, wig, wout),
        out_specs=[
            pl.BlockSpec((None, tm, d), lambda b, i: (b, i, 0)),
            pl.BlockSpec((None, CONV_WIDTH - 1, d), lambda b, i: (b, 0, 0)),
            pl.BlockSpec((None, 1, d), lambda b, i: (b, 0, 0)),
        ],
        out_shape=[
            jax.ShapeDtypeStruct((bsz, seq, d), F32),
            jax.ShapeDtypeStruct((bsz, CONV_WIDTH - 1, d), F32),
            jax.ShapeDtypeStruct((bsz, 1, d), F32),
        ],
        scratch_shapes=[
            pltpu.VMEM((nsub, sub, d), BF16),
            pltpu.VMEM((nsub, sub + head, d), F32),
            pltpu.VMEM((head, d), F32),
            pltpu.VMEM((nsub, sub, d), F32),
            pltpu.VMEM((nsub, sub, d), F32),
            pltpu.VMEM((nsub, sub, d), F32),
            pltpu.VMEM((nsub, sub, d), F32),
            pltpu.VMEM((nsub, sub, d), F32),
            pltpu.VMEM((nsub, SUBLANES, d), F32),
            pltpu.VMEM((nsub, SUBLANES, d), F32),
            pltpu.VMEM((nsub, sub, d), BF16),
            pltpu.VMEM((1, d), F32),
        ],
        compiler_params=_params(("arbitrary", "arbitrary")),
        name="rec_prompt",
    )(x, g.reshape(1, d), win, cw, cb.reshape(1, d), wrg, brg.reshape(1, d), wig,
      big.reshape(1, d), lam.reshape(1, d), wout)


def _rec_sample_kernel(x_ref, sc_ref, h0_ref, g_ref, win_ref, cw_ref, cb_ref, wrg_ref, brg_ref,
                       wig_ref, big_ref, lam_ref, wout_ref, y_ref, conv_ref, hout_ref,
                       a_scr, b_scr, hs_scr, *, t_new, nb):
    d = x_ref.shape[-1]
    x = x_ref[...]
    h = _rms(x, g_ref[...]).astype(BF16)
    gx = _dot(h, win_ref[...])
    gate = gx[:, :d]
    xb = gx[:, d:]
    xp = [sc_ref[j] for j in range(CONV_WIDTH - 1)] + [xb[t * nb:(t + 1) * nb, :] for t in range(t_new)]
    xcs = []
    for t in range(t_new):
        acc = cb_ref[...]
        for j in range(CONV_WIDTH):
            acc = acc + xp[t + j] * cw_ref[j:j + 1, :]
        xcs.append(acc)
    xc = jnp.concatenate(xcs, axis=0)

    _lru_coeffs(xc, wrg_ref, brg_ref, wig_ref, big_ref, lam_ref, a_scr, b_scr)

    hprev = h0_ref[...]
    for t in range(t_new):
        sl = slice(t * nb, (t + 1) * nb)
        hprev = a_scr[sl, :] * hprev + b_scr[sl, :]
        hs_scr[sl, :] = hprev
    y_ref[...] = x + _dot((hs_scr[...] * _gelu_tanh(gate)).astype(BF16), wout_ref[...])
    for j in range(CONV_WIDTH - 1):
        conv_ref[j] = xp[t_new + j]
    hout_ref[...] = hprev


def _rec_sample(x, sc, h0, g, win, cw, cb, wrg, brg, wig, big, lam, wout, *, t_new):
    n, d = x.shape
    nb = n // t_new
    kern = functools.partial(_rec_sample_kernel, t_new=t_new, nb=nb)
    return pl.pallas_call(
        kern,
        grid=(1,),
        in_specs=[_full((n, d)), _full(sc.shape), _full(h0.shape)]
        + _rec_weight_specs(d, win, wrg, wig, wout),
        out_specs=[_whole((n, d)), _whole(sc.shape), _whole(h0.shape)],
        out_shape=[
            jax.ShapeDtypeStruct((n, d), F32),
            jax.ShapeDtypeStruct(sc.shape, F32),
            jax.ShapeDtypeStruct(h0.shape, F32),
        ],
        scratch_shapes=[pltpu.VMEM((n, d), F32)] * 3,
        compiler_params=_params(("arbitrary",)),
        name="rec_sample",
    )(x, sc, h0, g.reshape(1, d), win, cw, cb.reshape(1, d), wrg, brg.reshape(1, d), wig,
      big.reshape(1, d), lam.reshape(1, d), wout)


def _pick_tile(seq, want):
    tm = min(want, seq)
    while seq % tm:
        tm //= 2
    return tm


def kernel(x_prompt, x_sample, cache_k, cache_v, state_conv, state_h, attn_norm, w_qkv, w_attn_out,
           attn_sinks, rec_norm, w_rec_in, conv_w, conv_b, w_rgate, b_rgate, w_igate, b_igate,
           lru_lambda, w_rec_out, ffn_norm, w_ffn_in, w_ffn_out, final_norm):
    bsz, seq, d = x_prompt.shape
    dbsz, t_new, _ = x_sample.shape
    depth = ffn_norm.shape[0]
    nh = attn_sinks.shape[1]
    hd = w_attn_out.shape[1] // nh
    nkv = (w_qkv.shape[2] // hd - nh) // 2
    nq, nk = nh * hd, nkv * hd
    win = cache_k.shape[2]
    tm = _pick_tile(seq, TILE_TOKENS)
    tm_ffn = _pick_tile(bsz * seq, FFN_TILE_TOKENS)
    sub = _pick_tile(min(tm, tm_ffn), REC_SUB_TOKENS)
    bb = _pick_tile(dbsz, 16)

    xp = x_prompt
    xs = x_sample.reshape(dbsz * t_new, d)
    time_major = False
    nk_p, nv_p, nk_s, nv_s, nc_p, nh_p, nc_s, nh_s = [], [], [], [], [], [], [], []

    wfi = w_ffn_in.astype(BF16)
    wfo = w_ffn_out.astype(BF16)
    for layer in range(depth):
        j = layer // 2
        gf = final_norm if layer == depth - 1 else None
        if layer % 2 == 0:
            wqkv = w_qkv[j].astype(BF16)
            wo = w_attn_out[j].astype(BF16)
            xp, kp, vp = _attn_prompt(xp, attn_norm[j], wqkv, wo, attn_sinks[j],
                                      tm=tm, nh=nh, nkv=nkv, hd=hd)
            nk_p.append(kp.reshape(bsz, BLOCK, nkv, hd))
            nv_p.append(vp.reshape(bsz, BLOCK, nkv, hd))

            if time_major:
                xs = xs.reshape(t_new, dbsz, d).transpose(1, 0, 2).reshape(dbsz * t_new, d)
                time_major = False
            q, kvt = _rms_matmul(xs, attn_norm[j], wqkv[:, :nq], wqkv[:, nq:].T)
            q3 = q.reshape(dbsz, t_new * nh, hd)
            ck = cache_k[j].transpose(0, 2, 3, 1).reshape(dbsz, nk, win)
            cv = cache_v[j].transpose(0, 2, 3, 1).reshape(dbsz, nk, win)
            o3, nks, nvs = _attn_sample(q3, kvt, ck, cv, attn_sinks[j], t_new=t_new, nh=nh, nkv=nkv, hd=hd,
                                        bb=bb)
            nk_s.append(nks.reshape(dbsz, nkv, hd, win).transpose(0, 3, 1, 2))
            nv_s.append(nvs.reshape(dbsz, nkv, hd, win).transpose(0, 3, 1, 2))
            sample_proj = dict(os=o3.reshape(dbsz * t_new, nq), wo=wo)
        else:
            rec_w = (rec_norm[j], w_rec_in[j].astype(BF16), conv_w[j], conv_b[j],
                     w_rgate[j].astype(BF16), b_rgate[j], w_igate[j].astype(BF16), b_igate[j],
                     lru_lambda[j], w_rec_out[j].astype(BF16))
            xp, cp, hp = _rec_prompt(xp, *rec_w, tm=tm, sub=sub)
            nc_p.append(cp)
            nh_p.append(hp.reshape(bsz, d))

            if not time_major:
                xs = xs.reshape(dbsz, t_new, d).transpose(1, 0, 2).reshape(t_new * dbsz, d)
                time_major = True
            sc = state_conv[j].transpose(1, 0, 2)
            xs, cs, hs = _rec_sample(xs, sc, state_h[j], *rec_w, t_new=t_new)
            nc_s.append(cs.transpose(1, 0, 2))
            nh_s.append(hs)
            sample_proj = {}
        xp, xs = _ffn(xp.reshape(bsz * seq, d), xs, ffn_norm[layer], wfi, wfo, layer, tm=tm_ffn, gf=gf,
                      perm_sub=sub, perm_in=layer % 2 == 1, perm_out=layer % 2 == 0 and layer + 1 < depth,
                      **sample_proj)
        xp = xp.reshape(bsz, seq, d)

    if time_major:
        ys = xs.reshape(t_new, dbsz, d).transpose(1, 0, 2)
    else:
        ys = xs.reshape(dbsz, t_new, d)
    return (xp, ys, jnp.stack(nk_p), jnp.stack(nv_p), jnp.stack(nk_s), jnp.stack(nv_s),
            jnp.stack(nc_p), jnp.stack(nh_p), jnp.stack(nc_s), jnp.stack(nh_s))
```

```python
import functools

import numpy as np

import jax
import jax.numpy as jnp
from jax import lax
from jax.experimental import pallas as pl
from jax.experimental.pallas import tpu as pltpu

EPS = 1e-6
NEG_INF = -1e30
LRU_C = 8.0
CONV_WIDTH = 4
N_LRU_BLOCKS = 4
BLOCK = 128
SUBLANES = 8
LANES = 128
SM_ROWS = 64
TILE_TOKENS = 1024
FFN_TILE_TOKENS = 1024
FFN_SUB_TOKENS = 512
ATTN_SUB_TOKENS = 512
REC_SUB_TOKENS = 512
LOG2E = 1.4426950408889634
VMEM_LIMIT = 56 * 1024 * 1024

F32 = jnp.float32
BF16 = jnp.bfloat16


def _rms(x, g):
    return x * lax.rsqrt(jnp.mean(x * x, axis=-1, keepdims=True) + EPS) * g


def _dot(a, b):
    return jnp.dot(a, b, preferred_element_type=F32)


def _dot_nt(a, b):
    return lax.dot_general(a, b, (((1,), (1,)), ((), ())), preferred_element_type=F32)


def _softplus(z):
    return jnp.maximum(z, 0.0) + jnp.log1p(jnp.exp(-jnp.abs(z)))


def _full(shape):
    nd = len(shape)
    return pl.BlockSpec(shape, lambda *_: (0,) * nd, pipeline_mode=pl.Buffered(1))


def _whole(shape):
    nd = len(shape)
    return pl.BlockSpec(shape, lambda *_: (0,) * nd)


def _params(sem, flags=None):
    return pltpu.CompilerParams(dimension_semantics=sem, vmem_limit_bytes=VMEM_LIMIT, flags=flags)


def _attn_prompt_kernel(x_ref, g_ref, wqkv_ref, wo_ref, bias_ref, sink_ref, sinkp_ref,
                        y_ref, kout_ref, vout_ref,
                        h_scr, q_scr, klo_scr, khi_scr, vlo_scr, vhi_scr, t_scr, m_scr, p_scr, o_scr,
                        *, tm, sub, nh, nkv, hd):
    i = pl.program_id(1)
    grp = nh // nkv
    nq, nk = nh * hd, nkv * hd
    half = (grp // 2) * BLOCK
    unit = 2 * half
    nsub = tm // sub
    lo = lax.broadcasted_iota(jnp.int32, (1, LANES), 1) < hd

    @pl.when(i == 0)
    def _():
        ones_lo = jnp.broadcast_to(jnp.where(lo, 1.0, 0.0).astype(BF16), (tm + BLOCK, LANES))
        ones_hi = jnp.broadcast_to(jnp.where(lo, 0.0, 1.0).astype(BF16), (tm + BLOCK, LANES))
        for kv in range(nkv):
            klo_scr[0:BLOCK, kv * LANES:(kv + 1) * LANES] = jnp.zeros((BLOCK, LANES), BF16)
            khi_scr[0:BLOCK, kv * LANES:(kv + 1) * LANES] = jnp.zeros((BLOCK, LANES), BF16)
            vlo_scr[0:BLOCK, 2 * kv * LANES:(2 * kv + 1) * LANES] = jnp.zeros((BLOCK, LANES), BF16)
            vhi_scr[0:BLOCK, 2 * kv * LANES:(2 * kv + 1) * LANES] = jnp.zeros((BLOCK, LANES), BF16)
            vlo_scr[:, (2 * kv + 1) * LANES:(2 * kv + 2) * LANES] = ones_lo
            vhi_scr[:, (2 * kv + 1) * LANES:(2 * kv + 2) * LANES] = ones_hi

    def qkv_ops(t):
        tok = slice(t * sub, (t + 1) * sub)
        rows = slice(BLOCK + t * sub, BLOCK + (t + 1) * sub)

        def norm():
            h_scr[tok, :] = _rms(x_ref[tok, :], g_ref[...]).astype(BF16)

        def q_op(c):
            cols = slice(c * nk, (c + 1) * nk)
            q_scr[tok, cols] = (_dot(h_scr[tok, :], wqkv_ref[:, cols]) * (hd ** -0.5 * LOG2E)).astype(BF16)

        def kv_op(which):
            lo_scr, hi_scr, step, out_ref = ((klo_scr, khi_scr, 1, kout_ref), (vlo_scr, vhi_scr, 2, vout_ref))[which]
            val = _dot(h_scr[tok, :], wqkv_ref[:, nq + which * nk:nq + (which + 1) * nk])
            if t == nsub - 1:
                out_ref[...] = val[sub - BLOCK:, :]
            for c in range(nk // LANES):
                chunk = val[:, c * LANES:(c + 1) * LANES]
                swapped = pltpu.roll(chunk, hd, axis=1)
                even = slice(2 * c * step * LANES, (2 * c * step + 1) * LANES)
                odd = slice((2 * c + 1) * step * LANES, ((2 * c + 1) * step + 1) * LANES)
                lo_scr[rows, even] = jnp.where(lo, chunk, 0.0).astype(BF16)
                hi_scr[rows, even] = jnp.where(lo, 0.0, swapped).astype(BF16)
                lo_scr[rows, odd] = jnp.where(lo, swapped, 0.0).astype(BF16)
                hi_scr[rows, odd] = jnp.where(lo, 0.0, chunk).astype(BF16)

        return ([norm] + [functools.partial(q_op, c) for c in range(nq // nk)]
                + [functools.partial(kv_op, which) for which in range(2)])

    def scores(j, buf):
        sel = jnp.where(i == 0, 1, 0) if j == 0 else 0
        r0 = j * BLOCK
        for kv in range(nkv):
            qp = jnp.concatenate(
                [q_scr[r0:r0 + BLOCK, (kv * grp + 2 * pr) * hd:(kv * grp + 2 * pr + 2) * hd]
                 for pr in range(grp // 2)], axis=0)
            win = (slice(r0, r0 + 2 * BLOCK), slice(kv * LANES, (kv + 1) * LANES))
            for par, k_scr in enumerate((klo_scr, khi_scr)):
                rs = slice(kv * unit + par * half, kv * unit + (par + 1) * half)
                t = _dot_nt(qp, k_scr[win]) + bias_ref[sel, rs, :]
                t_scr[buf, rs, :] = t
                m_scr[buf, rs, :] = jnp.maximum(jnp.max(t, axis=-1, keepdims=True), sink_ref[rs, :])

    def probs(j, buf):
        for c in range(nkv * unit // SM_ROWS):
            rs = slice(c * SM_ROWS, (c + 1) * SM_ROWS)
            m = m_scr[buf, rs, :]
            p_scr[buf, rs, :] = jnp.concatenate(
                [jnp.exp2(t_scr[buf, rs, q * LANES:(q + 1) * LANES] - m)
                 for q in range(2 * BLOCK // LANES)], axis=1).astype(BF16)

    def values(j, buf):
        r0 = j * BLOCK
        for kv in range(nkv):
            win = (slice(r0, r0 + 2 * BLOCK), slice(2 * kv * LANES, (2 * kv + 2) * LANES))
            ev = slice(kv * unit, kv * unit + half)
            od = slice(kv * unit + half, (kv + 1) * unit)
            ol = _dot(p_scr[buf, ev, :], vlo_scr[win]) + _dot(p_scr[buf, od, :], vhi_scr[win])
            m = jnp.where(lo, m_scr[buf, ev, :], m_scr[buf, od, :])
            denom = ol[:, LANES:] + jnp.exp2(sinkp_ref[kv * half:(kv + 1) * half, :] - m)
            o = ol[:, :LANES] * (1.0 / denom)
            for pr in range(grp // 2):
                c0 = (kv * grp + 2 * pr) * hd
                o_scr[r0:r0 + BLOCK, c0:c0 + LANES] = o[pr * BLOCK:(pr + 1) * BLOCK, :].astype(BF16)

    def attn_ops(t):
        blocks = range(t * sub // BLOCK, (t + 1) * sub // BLOCK)
        ops = [functools.partial(scores, blocks[0], blocks[0] % 2)]
        for j in blocks:
            if j + 1 in blocks:
                ops.append(functools.partial(scores, j + 1, (j + 1) % 2))
            ops += [functools.partial(probs, j, j % 2), functools.partial(values, j, j % 2)]
        return ops

    def out_ops(t):
        tok = slice(t * sub, (t + 1) * sub)

        def op(c):
            cols = slice(c * nk, (c + 1) * nk)
            y_ref[tok, cols] = x_ref[tok, cols] + _dot(o_scr[tok, :], wo_ref[:, cols])
        return [functools.partial(op, c) for c in range(y_ref.shape[1] // nk)]

    stages = (qkv_ops, attn_ops, out_ops)
    for phase in range(nsub + len(stages) - 1):
        active = [stages[phase - t](t) for t in range(nsub) if 0 <= phase - t < len(stages)]
        for op in _interleave(*active):
            op()

    for kv in range(nkv):
        for k_scr, v_scr in ((klo_scr, vlo_scr), (khi_scr, vhi_scr)):
            k_scr[0:BLOCK, kv * LANES:(kv + 1) * LANES] = k_scr[tm:tm + BLOCK, kv * LANES:(kv + 1) * LANES]
            v_scr[0:BLOCK, 2 * kv * LANES:(2 * kv + 1) * LANES] = v_scr[tm:tm + BLOCK, 2 * kv * LANES:(2 * kv + 1) * LANES]


def _alibi_slopes(nh):
    return (2.0 ** (-8.0 * np.arange(1, nh + 1, dtype=np.float32) / nh)).astype(np.float32)


def _attn_prompt(x, g, wqkv, wo, sinks, *, tm, nh, nkv, hd):
    bsz, seq, d = x.shape
    grp = nh // nkv
    nk = nkv * hd
    assert 2 * hd == LANES and grp % 2 == 0 and tm % BLOCK == 0
    half = grp // 2 * BLOCK
    r = np.arange(grp * BLOCK)
    head_in_kv = 2 * ((r % half) // BLOCK) + r // half
    head = (np.arange(nkv)[:, None] * grp + head_in_kv[None, :]).reshape(-1)
    nrows = nkv * grp * BLOCK
    qi = np.tile(np.arange(BLOCK), nkv * grp)[:, None]
    si = np.arange(2 * BLOCK)[None, :]
    dist = qi - si + BLOCK
    band = (dist >= 0) & (dist <= BLOCK)
    slopes = _alibi_slopes(nh)[head][:, None]
    bias = np.where(band, -slopes * dist.astype(np.float32) * np.float32(LOG2E), np.float32(NEG_INF))
    bias = np.stack([bias, np.where(si < BLOCK, np.float32(NEG_INF), bias)]).astype(np.float32)
    sink2 = (sinks.astype(F32) * LOG2E).reshape(nkv, grp // 2, 2)
    sink = jnp.broadcast_to(sink2.transpose(0, 2, 1)[:, :, :, None, None],
                            (nkv, 2, grp // 2, BLOCK, LANES)).reshape(nrows, LANES)
    sink_pair = jnp.broadcast_to(jnp.repeat(sink2, hd, axis=2)[:, :, None, :],
                                 (nkv, grp // 2, BLOCK, LANES)).reshape(nkv * half, LANES)

    sub = _pick_tile(tm, ATTN_SUB_TOKENS)
    assert sub % BLOCK == 0 and (nh * hd) % nk == 0 and d % nk == 0
    kern = functools.partial(_attn_prompt_kernel, tm=tm, sub=sub, nh=nh, nkv=nkv, hd=hd)
    return pl.pallas_call(
        kern,
        grid=(bsz, seq // tm),
        in_specs=[
            pl.BlockSpec((None, tm, d), lambda b, i: (b, i, 0)),
            _full((1, d)),
            _full(wqkv.shape),
            _full(wo.shape),
            _full(bias.shape),
            _full(sink.shape),
            _full(sink_pair.shape),
        ],
        out_specs=[
            pl.BlockSpec((None, tm, d), lambda b, i: (b, i, 0)),
            pl.BlockSpec((None, BLOCK, nk), lambda b, i: (b, 0, 0)),
            pl.BlockSpec((None, BLOCK, nk), lambda b, i: (b, 0, 0)),
        ],
        out_shape=[
            jax.ShapeDtypeStruct((bsz, seq, d), F32),
            jax.ShapeDtypeStruct((bsz, BLOCK, nk), F32),
            jax.ShapeDtypeStruct((bsz, BLOCK, nk), F32),
        ],
        scratch_shapes=[
            pltpu.VMEM((tm, d), BF16),
            pltpu.VMEM((tm, nh * hd), BF16),
            pltpu.VMEM((tm + BLOCK, nkv * LANES), BF16),
            pltpu.VMEM((tm + BLOCK, nkv * LANES), BF16),
            pltpu.VMEM((tm + BLOCK, 2 * nkv * LANES), BF16),
            pltpu.VMEM((tm + BLOCK, 2 * nkv * LANES), BF16),
            pltpu.VMEM((2, nrows, 2 * BLOCK), F32),
            pltpu.VMEM((2, nrows, LANES), F32),
            pltpu.VMEM((2, nrows, 2 * BLOCK), BF16),
            pltpu.VMEM((tm, nh * hd), BF16),
        ],
        compiler_params=_params(("arbitrary", "arbitrary")),
        name="attn_prompt",
    )(x, g.reshape(1, d), wqkv, wo, bias, sink, sink_pair)


def _rms_matmul_kernel(x_ref, g_ref, w_ref, wt_ref, y_ref, yt_ref):
    h = _rms(x_ref[...], g_ref[...]).astype(BF16)
    y_ref[...] = _dot(h, w_ref[...])
    yt_ref[...] = _dot_nt(wt_ref[...], h)


def _rms_matmul(x, g, w, wt):
    n, d = x.shape
    return pl.pallas_call(
        _rms_matmul_kernel,
        grid=(1,),
        in_specs=[_full((n, d)), _full((1, d)), _full(w.shape), _full(wt.shape)],
        out_specs=[_whole((n, w.shape[1])), _whole((wt.shape[0], n))],
        out_shape=[jax.ShapeDtypeStruct((n, w.shape[1]), F32), jax.ShapeDtypeStruct((wt.shape[0], n), F32)],
        compiler_params=_params(("arbitrary",)),
        name="rms_matmul",
    )(x, g.reshape(1, d), w, wt)


def _attn_sample_kernel(q_ref, kvt_ref, ck_ref, cv_ref, e_ref, et_ref, mask_ref,
                        bias_ref, sink_ref, o_ref, nk_ref, nv_ref,
                        qx_scr, kk_scr, vv_scr, s_scr, p_scr, ox_scr,
                        *, bb, rows, t_new, hd):
    nk, win = ck_ref.shape[1], ck_ref.shape[2]
    lane = lax.broadcasted_iota(jnp.int32, (1, win), 1)
    first_lane = (pl.program_id(0) * (bb * t_new)) % LANES

    q = (q_ref[...].reshape(bb * rows, hd) * (hd ** -0.5 * LOG2E)).astype(BF16)
    mask = mask_ref[...]
    qx_scr[...] = (_dot(q, e_ref[...]) * mask).astype(BF16)
    for b in range(bb):
        new = pltpu.roll(kvt_ref[...], (2 * LANES - t_new - first_lane - b * t_new) % LANES, axis=1)
        new = jnp.where(lane >= win - t_new, new, 0.0)
        for c_ref, scr, out_ref, part in ((ck_ref, kk_scr, nk_ref, new[:nk]), (cv_ref, vv_scr, nv_ref, new[nk:])):
            old = c_ref[b]
            scr[b, :, 0:win] = old.astype(BF16)
            scr[b, :, win:2 * win] = part.astype(BF16)
            out_ref[b] = jnp.where(lane < win - t_new, pltpu.roll(old, win - t_new, axis=1), part)
    for b in range(bb):
        rs = slice(b * rows, (b + 1) * rows)
        s_scr[rs, :] = _dot(qx_scr[rs, :], kk_scr[b])

    t = s_scr[...] + bias_ref[...]
    sink = sink_ref[...]
    m = jnp.maximum(jnp.max(t, axis=-1, keepdims=True), sink)
    p = jnp.concatenate([jnp.exp2(t[:, c * LANES:(c + 1) * LANES] - m)
                         for c in range(2 * win // LANES)], axis=1)
    p_scr[...] = p.astype(BF16)
    inv = 1.0 / (jnp.sum(p, axis=-1, keepdims=True) + jnp.exp2(sink - m))
    scale = jnp.concatenate([inv] * (mask.shape[1] // LANES), axis=1) * mask
    for b in range(bb):
        rs = slice(b * rows, (b + 1) * rows)
        ox_scr[rs, :] = (_dot_nt(p_scr[rs, :], vv_scr[b]) * scale[rs, :]).astype(BF16)
    o_ref[...] = _dot(ox_scr[...], et_ref[...]).reshape(bb, rows, hd)


def _attn_sample(q3, kvt, ck, cv, sinks, *, t_new, nh, nkv, hd, bb):
    dbsz, rows, _ = q3.shape
    nk, win = ck.shape[1], ck.shape[2]
    grp = nh // nkv
    assert win == LANES and nk == nkv * hd and LANES % (bb * t_new) == 0 and kvt.shape[1] % LANES == 0
    row_t = np.arange(rows)[:, None] // nh
    row_h = np.arange(rows) % nh
    row_slope = _alibi_slopes(nh)[row_h][:, None]
    col = np.arange(2 * win)[None, :]
    key = np.where(col < win, col, col - (win - t_new))
    dist = (row_t - key + win).astype(np.float32)
    valid = ((col < win) | (col >= 2 * win - t_new)) & (key >= row_t) & (key <= row_t + win)
    bias = np.where(valid, -row_slope * dist * np.float32(LOG2E), np.float32(NEG_INF)).astype(np.float32)
    bias = np.tile(bias, (bb, 1))
    mask = (np.arange(nk)[None, :] // hd == (row_h // grp)[:, None]).astype(np.float32)
    mask = np.tile(mask, (bb, 1))
    e = np.tile(np.eye(hd, dtype=np.float32), (1, nkv))
    sink = jnp.broadcast_to(jnp.tile(sinks.astype(F32) * LOG2E, bb * t_new)[:, None], (bb * rows, LANES))

    kern = functools.partial(_attn_sample_kernel, bb=bb, rows=rows, t_new=t_new, hd=hd)
    return pl.pallas_call(
        kern,
        grid=(dbsz // bb,),
        in_specs=[
            pl.BlockSpec((bb, rows, hd), lambda i: (i, 0, 0)),
            pl.BlockSpec((2 * nk, LANES), lambda i: (0, (i * bb * t_new) // LANES)),
            pl.BlockSpec((bb, nk, win), lambda i: (i, 0, 0)),
            pl.BlockSpec((bb, nk, win), lambda i: (i, 0, 0)),
            _full(e.shape), _full(e.T.shape), _full(mask.shape),
            _full(bias.shape), _full(sink.shape),
        ],
        out_specs=[
            pl.BlockSpec((bb, rows, hd), lambda i: (i, 0, 0)),
            pl.BlockSpec((bb, nk, win), lambda i: (i, 0, 0)),
            pl.BlockSpec((bb, nk, win), lambda i: (i, 0, 0)),
        ],
        out_shape=[
            jax.ShapeDtypeStruct((dbsz, rows, hd), F32),
            jax.ShapeDtypeStruct((dbsz, nk, win), F32),
            jax.ShapeDtypeStruct((dbsz, nk, win), F32),
        ],
        scratch_shapes=[
            pltpu.VMEM((bb * rows, nk), BF16),
            pltpu.VMEM((bb, nk, 2 * win), BF16),
            pltpu.VMEM((bb, nk, 2 * win), BF16),
            pltpu.VMEM((bb * rows, 2 * win), F32),
            pltpu.VMEM((bb * rows, 2 * win), BF16),
            pltpu.VMEM((bb * rows, nk), BF16),
        ],
        compiler_params=_params(("arbitrary",)),
        name="attn_sample",
    )(q3, kvt, ck, cv, jnp.asarray(e, BF16), jnp.asarray(e.T, BF16), mask, bias, sink)


def _time_permute(v, sub, inverse):
    n, d = v.shape
    ns = sub // SUBLANES
    shape = (ns, SUBLANES, d) if inverse else (SUBLANES, ns, d)
    parts = [v[t * sub:(t + 1) * sub, :].reshape(shape).swapaxes(0, 1).reshape(sub, d)
             for t in range(n // sub)]
    return parts[0] if len(parts) == 1 else jnp.concatenate(parts, axis=0)


def _ffn_tile(x_ref, y_ref, o_ref, wo_ref, g_ref, win_ref, wout_ref, gf_ref, h_scr, a_scr, *, dff, chunk,
              sub_rows=None, perm_sub=None, perm_in=False, perm_out=False):
    proj, final = o_ref is not None, gf_ref is not None
    res_ref = x_ref
    if proj:
        y_ref[...] = x_ref[...] + _dot(o_ref[...].astype(BF16), wo_ref[...])
        res_ref = y_ref
    n, d = y_ref.shape
    sub = sub_rows if sub_rows is not None and n % sub_rows == 0 else n
    nsub = n // sub
    sumsq = [[] for _ in range(nsub)]

    def norm_ops(t):
        rows = slice(t * sub, (t + 1) * sub)

        def op():
            h_scr[rows, :] = _rms(res_ref[rows, :], g_ref[...]).astype(BF16)
        return [op]

    def up_ops(t):
        rows = slice(t * sub, (t + 1) * sub)

        def op(c):
            gate = _dot(h_scr[rows, :], win_ref[:, c * chunk:(c + 1) * chunk])
            up = _dot(h_scr[rows, :], win_ref[:, dff + c * chunk:dff + (c + 1) * chunk])
            a_scr[rows, c * chunk:(c + 1) * chunk] = (gate * jax.nn.sigmoid(gate) * up).astype(BF16)
        return [functools.partial(op, c) for c in range(dff // chunk)]

    def down_ops(t):
        rows = slice(t * sub, (t + 1) * sub)

        def op(c):
            cols = slice(c * chunk, (c + 1) * chunk)
            y = res_ref[rows, cols] + _dot(a_scr[rows, :], wout_ref[:, cols])
            if perm_in or perm_out:
                y = _time_permute(y, perm_sub, inverse=perm_in)
            if final:
                sumsq[t].append(jnp.sum(y * y, axis=-1, keepdims=True))
            y_ref[rows, cols] = y

        def scale():
            y_ref[rows, :] = y_ref[rows, :] * lax.rsqrt(sum(sumsq[t]) * (1.0 / d) + EPS) * gf_ref[...]
        return [functools.partial(op, c) for c in range(d // chunk)] + ([scale] if final else [])

    stages = (norm_ops, up_ops, down_ops)
    for phase in range(nsub + len(stages) - 1):
        active = [stages[phase - t](t) for t in range(nsub) if 0 <= phase - t < len(stages)]
        for op in _interleave(*active):
            op()


def _ffn_kernel(*refs, nt, proj, final, **tile_kw):
    refs = list(refs)
    x_ref, xs_ref = refs.pop(0), refs.pop(0)
    os_ref, wo_ref = (refs.pop(0), refs.pop(0)) if proj else (None, None)
    g_ref, win_ref, wout_ref = refs.pop(0), refs.pop(0), refs.pop(0)
    gf_ref = refs.pop(0) if final else None
    y_ref, ys_ref, h_scr, a_scr = refs
    step = pl.program_id(0)

    @pl.when(step < nt)
    def _():
        _ffn_tile(x_ref, y_ref, None, None, g_ref, win_ref, wout_ref, gf_ref, h_scr, a_scr, **tile_kw)

    @pl.when(step == nt)
    def _():
        rows = pl.ds(0, xs_ref.shape[0])
        _ffn_tile(xs_ref, ys_ref, os_ref, wo_ref, g_ref, win_ref, wout_ref, gf_ref,
                  h_scr.at[rows], a_scr.at[rows], dff=tile_kw["dff"], chunk=tile_kw["chunk"])


def _layer_of(stacked, layer):
    nd = stacked.ndim - 1
    return pl.BlockSpec((None,) + stacked.shape[1:], lambda *_: (layer,) + (0,) * nd,
                        pipeline_mode=pl.Buffered(1))


def _ffn(x, xs, g, win_all, wout_all, layer, *, tm, os=None, wo=None, gf=None, chunk=256, perm_sub=None,
         perm_in=False, perm_out=False):
    n, d = x.shape
    ns = xs.shape[0]
    nt = n // tm
    dff = wout_all.shape[1]
    proj, final = os is not None, gf is not None
    assert not (perm_in and perm_out) and (perm_sub is None or tm % perm_sub == 0) and ns <= tm
    tile = pl.BlockSpec((tm, d), lambda i: (jnp.minimum(i, nt - 1), 0))
    args, specs = [x, xs], [tile, _full(xs.shape)]
    if proj:
        args += [os, wo]
        specs += [_full(os.shape), _full(wo.shape)]
    args += [g.reshape(1, d), win_all, wout_all]
    specs += [_full((1, d)), _layer_of(win_all, layer), _layer_of(wout_all, layer)]
    if final:
        args.append(gf.reshape(1, d))
        specs.append(_full((1, d)))
    kern = functools.partial(_ffn_kernel, nt=nt, proj=proj, final=final, dff=dff, chunk=chunk,
                             sub_rows=FFN_SUB_TOKENS, perm_sub=perm_sub, perm_in=perm_in, perm_out=perm_out)
    return pl.pallas_call(
        kern,
        grid=(nt + 1,),
        in_specs=specs,
        out_specs=[tile, _whole(xs.shape)],
        out_shape=[jax.ShapeDtypeStruct((n, d), F32), jax.ShapeDtypeStruct(xs.shape, F32)],
        scratch_shapes=[pltpu.VMEM((tm, d), BF16), pltpu.VMEM((tm, dff), BF16)],
        compiler_params=_params(("arbitrary",)),
        name="ffn",
    )(*args)


def _lru_block_coeffs(xcn, n, sl, wrg_ref, brg_ref, wig_ref, big_ref, lam_ref):
    xcb = xcn.astype(BF16)
    r = jax.nn.sigmoid(_dot(xcb, wrg_ref[n]) + brg_ref[:, sl])
    ig = jax.nn.sigmoid(_dot(xcb, wig_ref[n]) + big_ref[:, sl])
    log_a = r * (-LRU_C * _softplus(-lam_ref[:, sl]))
    a = jnp.exp(log_a)
    gain2 = -jnp.tanh(log_a) * (a * a + 1.0)
    gain = jnp.where(gain2 > 0.0, gain2 * lax.rsqrt(gain2), 0.0)
    return a, gain * (ig * xcn)


def _lru_coeffs(xc, wrg_ref, brg_ref, wig_ref, big_ref, lam_ref, a_out, b_out):
    blk = xc.shape[1] // N_LRU_BLOCKS
    for n in range(N_LRU_BLOCKS):
        sl = slice(n * blk, (n + 1) * blk)
        a_out[:, sl], b_out[:, sl] = _lru_block_coeffs(xc[:, sl], n, sl, wrg_ref, brg_ref, wig_ref,
                                                       big_ref, lam_ref)


def _interleave(*op_lists):
    keyed = [((k + 0.5) / len(ops), which, k, op)
             for which, ops in enumerate(op_lists) for k, op in enumerate(ops)]
    return [op for _, _, _, op in sorted(keyed, key=lambda e: e[:3])]


def _gelu_tanh(x):
    c0 = 0.7978845608028654
    inner = x * (c0 + (c0 * 0.044715) * (x * x))
    hx = 0.5 * x
    return hx + hx * jnp.tanh(inner)


def _rec_prompt_kernel(x_ref, g_ref, win_ref, cw_ref, cb_ref, wrg_ref, brg_ref, wig_ref, big_ref,
                       lam_ref, wout_ref, y_ref, conv_ref, hout_ref,
                       h_scr, xb_scr, tail_scr, gate_scr, a_scr, b_scr, hl_scr, pl_scr, hend_scr, pend_scr,
                       hg_scr, carry_scr, *, tm, sub):
    i = pl.program_id(1)
    last = pl.num_programs(1) - 1
    d = x_ref.shape[-1]
    nsub = tm // sub
    ns = sub // SUBLANES
    npre = CONV_WIDTH - 1
    head = npre * SUBLANES
    seg = lax.broadcasted_iota(jnp.int32, (SUBLANES, 1), 0)

    @pl.when(i == 0)
    def _():
        tail_scr[...] = jnp.zeros((head, d), F32)
        carry_scr[...] = jnp.zeros((1, d), F32)

    blk = d // N_LRU_BLOCKS

    def load_ops(t):
        def op():
            h_scr[t] = _rms(x_ref[t * sub:(t + 1) * sub, :], g_ref[...]).astype(BF16)
        return [op]

    def project_ops(t):
        def op(c):
            cols = slice(c * blk, (c + 1) * blk)
            part = _dot(h_scr[t], win_ref[:, cols])
            if c * blk < d:
                gate_scr[t, :, cols] = _gelu_tanh(part)
            else:
                xb_scr[t, head:head + sub, c * blk - d:(c + 1) * blk - d] = part
        return [functools.partial(op, c) for c in range(2 * d // blk)]

    def coeffs_ops(t):
        def heads():
            for k in range(npre):
                rows = slice(k * SUBLANES, (k + 1) * SUBLANES)
                tail = slice(sub + k * SUBLANES, sub + (k + 1) * SUBLANES)
                before = tail_scr[rows, :] if t == 0 else xb_scr[t - 1, tail, :]
                xb_scr[t, rows, :] = jnp.where(seg == 0, pltpu.roll(before, 1, axis=0),
                                               pltpu.roll(xb_scr[t, tail, :], 1, axis=0))

        def op(n):
            sl = slice(n * blk, (n + 1) * blk)
            xc = cb_ref[:, sl]
            for j in range(CONV_WIDTH):
                xc = xc + xb_scr[t, j * SUBLANES:j * SUBLANES + sub, sl] * cw_ref[j:j + 1, sl]
            a_scr[t, :, sl], b_scr[t, :, sl] = _lru_block_coeffs(xc, n, sl, wrg_ref, brg_ref, wig_ref,
                                                                 big_ref, lam_ref)
            hloc = b_scr[t, 0:SUBLANES, sl]
            prod = a_scr[t, 0:SUBLANES, sl]
            hl_scr[t, 0:SUBLANES, sl] = hloc
            pl_scr[t, 0:SUBLANES, sl] = prod
            for s in range(1, ns):
                rows = slice(s * SUBLANES, (s + 1) * SUBLANES)
                a = a_scr[t, rows, sl]
                hloc = a * hloc + b_scr[t, rows, sl]
                prod = a * prod
                hl_scr[t, rows, sl] = hloc
                pl_scr[t, rows, sl] = prod
            hend_scr[t, :, sl] = hloc
            pend_scr[t, :, sl] = prod
        return [heads] + [functools.partial(op, n) for n in range(N_LRU_BLOCKS)]

    def finish_ops(t):
        rows = slice(t * sub, (t + 1) * sub)

        def fix():
            hend, pend = hend_scr[t], pend_scr[t]
            a = jnp.where(seg == 0, 0.0, pltpu.roll(pend, 1, axis=0))
            b = jnp.where(seg == 0, carry_scr[...], pltpu.roll(hend, 1, axis=0))
            shift = 1
            while shift < SUBLANES:
                keep = seg >= shift
                b = jnp.where(keep, a * pltpu.roll(b, shift, axis=0) + b, b)
                a = jnp.where(keep, a * pltpu.roll(a, shift, axis=0), a)
                shift *= 2
            hstart = b
            carry_scr[...] = (hend + pend * hstart)[SUBLANES - 1:SUBLANES, :]
            hs = hl_scr[t].reshape(ns, SUBLANES, d) + pl_scr[t].reshape(ns, SUBLANES, d) * hstart[None]
            hg_scr[t] = (hs.reshape(sub, d) * gate_scr[t]).astype(BF16)

        def op(c):
            cols = slice(c * blk, (c + 1) * blk)
            y_ref[rows, cols] = x_ref[rows, cols] + _dot(hg_scr[t], wout_ref[:, cols])
        return [fix] + [functools.partial(op, c) for c in range(d // blk)]

    stages = (load_ops, project_ops, coeffs_ops, finish_ops)
    for phase in range(nsub + len(stages) - 1):
        active = [stages[phase - t](t) for t in range(nsub) if 0 <= phase - t < len(stages)]
        for op in _interleave(*active):
            op()

    @pl.when(i == last)
    def _():
        for k in range(npre):
            r = head + (ns - npre + k) * SUBLANES + SUBLANES - 1
            conv_ref[k:k + 1, :] = xb_scr[nsub - 1, r:r + 1, :]
        hout_ref[...] = carry_scr[...]

    tail_scr[...] = xb_scr[nsub - 1, sub:sub + head, :]


def _rec_weight_specs(d, win, wrg, wig, wout):
    return [_full((1, d)), _full(win.shape), _full((CONV_WIDTH, d)), _full((1, d)),
            _full(wrg.shape), _full((1, d)), _full(wig.shape), _full((1, d)), _full((1, d)),
            _full(wout.shape)]


def _rec_prompt(x, g, win, cw, cb, wrg, brg, wig, big, lam, wout, *, tm, sub):
    bsz, seq, d = x.shape
    nsub = tm // sub
    kern = functools.partial(_rec_prompt_kernel, tm=tm, sub=sub)
    head = (CONV_WIDTH - 1) * SUBLANES
    assert sub // SUBLANES >= CONV_WIDTH - 1
    return pl.pallas_call(
        kern,
        grid=(bsz, seq // tm),
        in_specs=[pl.BlockSpec((None, tm, d), lambda b, i: (b, i, 0))]
        + _rec_weight_specs(d, win, wrg, wig, wout),
        out_specs=[
            pl.BlockSpec((None, tm, d), lambda b, i: (b, i, 0)),
            pl.BlockSpec((None, CONV_WIDTH - 1, d), lambda b, i: (b, 0, 0)),
            pl.BlockSpec((None, 1, d), lambda b, i: (b, 0, 0)),
        ],
        out_shape=[
            jax.ShapeDtypeStruct((bsz, seq, d), F32),
            jax.ShapeDtypeStruct((bsz, CONV_WIDTH - 1, d), F32),
            jax.ShapeDtypeStruct((bsz, 1, d), F32),
        ],
        scratch_shapes=[
            pltpu.VMEM((nsub, sub, d), BF16),
            pltpu.VMEM((nsub, sub + head, d), F32),
            pltpu.VMEM((head, d), F32),
            pltpu.VMEM((nsub, sub, d), F32),
            pltpu.VMEM((nsub, sub, d), F32),
            pltpu.VMEM((nsub, sub, d), F32),
            pltpu.VMEM((nsub, sub, d), F32),
            pltpu.VMEM((nsub, sub, d), F32),
            pltpu.VMEM((nsub, SUBLANES, d), F32),
            pltpu.VMEM((nsub, SUBLANES, d), F32),
            pltpu.VMEM((nsub, sub, d), BF16),
            pltpu.VMEM((1, d), F32),
        ],
        compiler_params=_params(("arbitrary", "arbitrary")),
        name="rec_prompt",
    )(x, g.reshape(1, d), win, cw, cb.reshape(1, d), wrg, brg.reshape(1, d), wig,
      big.reshape(1, d), lam.reshape(1, d), wout)


def _rec_sample_kernel(x_ref, sc_ref, h0_ref, g_ref, win_ref, cw_ref, cb_ref, wrg_ref, brg_ref,
                       wig_ref, big_ref, lam_ref, wout_ref, y_ref, conv_ref, hout_ref,
                       a_scr, b_scr, hs_scr, *, t_new, nb):
    d = x_ref.shape[-1]
    x = x_ref[...]
    h = _rms(x, g_ref[...]).astype(BF16)
    gx = _dot(h, win_ref[...])
    gate = gx[:, :d]
    xb = gx[:, d:]
    xp = [sc_ref[j] for j in range(CONV_WIDTH - 1)] + [xb[t * nb:(t + 1) * nb, :] for t in range(t_new)]
    xcs = []
    for t in range(t_new):
        acc = cb_ref[...]
        for j in range(CONV_WIDTH):
            acc = acc + xp[t + j] * cw_ref[j:j + 1, :]
        xcs.append(acc)
    xc = jnp.concatenate(xcs, axis=0)

    _lru_coeffs(xc, wrg_ref, brg_ref, wig_ref, big_ref, lam_ref, a_scr, b_scr)

    hprev = h0_ref[...]
    for t in range(t_new):
        sl = slice(t * nb, (t + 1) * nb)
        hprev = a_scr[sl, :] * hprev + b_scr[sl, :]
        hs_scr[sl, :] = hprev
    y_ref[...] = x + _dot((hs_scr[...] * _gelu_tanh(gate)).astype(BF16), wout_ref[...])
    for j in range(CONV_WIDTH - 1):
        conv_ref[j] = xp[t_new + j]
    hout_ref[...] = hprev


def _rec_sample(x, sc, h0, g, win, cw, cb, wrg, brg, wig, big, lam, wout, *, t_new):
    n, d = x.shape
    nb = n // t_new
    kern = functools.partial(_rec_sample_kernel, t_new=t_new, nb=nb)
    return pl.pallas_call(
        kern,
        grid=(1,),
        in_specs=[_full((n, d)), _full(sc.shape), _full(h0.shape)]
        + _rec_weight_specs(d, win, wrg, wig, wout),
        out_specs=[_whole((n, d)), _whole(sc.shape), _whole(h0.shape)],
        out_shape=[
            jax.ShapeDtypeStruct((n, d), F32),
            jax.ShapeDtypeStruct(sc.shape, F32),
            jax.ShapeDtypeStruct(h0.shape, F32),
        ],
        scratch_shapes=[pltpu.VMEM((n, d), F32)] * 3,
        compiler_params=_params(("arbitrary",)),
        name="rec_sample",
    )(x, sc, h0, g.reshape(1, d), win, cw, cb.reshape(1, d), wrg, brg.reshape(1, d), wig,
      big.reshape(1, d), lam.reshape(1, d), wout)


def _pick_tile(seq, want):
    tm = min(want, seq)
    while seq % tm:
        tm //= 2
    return tm


def kernel(x_prompt, x_sample, cache_k, cache_v, state_conv, state_h, attn_norm, w_qkv, w_attn_out,
           attn_sinks, rec_norm, w_rec_in, conv_w, conv_b, w_rgate, b_rgate, w_igate, b_igate,
           lru_lambda, w_rec_out, ffn_norm, w_ffn_in, w_ffn_out, final_norm):
    bsz, seq, d = x_prompt.shape
    dbsz, t_new, _ = x_sample.shape
    depth = ffn_norm.shape[0]
    nh = attn_sinks.shape[1]
    hd = w_attn_out.shape[1] // nh
    nkv = (w_qkv.shape[2] // hd - nh) // 2
    nq, nk = nh * hd, nkv * hd
    win = cache_k.shape[2]
    tm = _pick_tile(seq, TILE_TOKENS)
    tm_ffn = _pick_tile(bsz * seq, FFN_TILE_TOKENS)
    sub = _pick_tile(min(tm, tm_ffn), REC_SUB_TOKENS)
    bb = _pick_tile(dbsz, 16)

    xp = x_prompt
    xs = x_sample.reshape(dbsz * t_new, d)
    time_major = False
    nk_p, nv_p, nk_s, nv_s, nc_p, nh_p, nc_s, nh_s = [], [], [], [], [], [], [], []

    wfi = w_ffn_in.astype(BF16)
    wfo = w_ffn_out.astype(BF16)
    for layer in range(depth):
        j = layer // 2
        gf = final_norm if layer == depth - 1 else None
        if layer % 2 == 0:
            wqkv = w_qkv[j].astype(BF16)
            wo = w_attn_out[j].astype(BF16)
            xp, kp, vp = _attn_prompt(xp, attn_norm[j], wqkv, wo, attn_sinks[j],
                                      tm=tm, nh=nh, nkv=nkv, hd=hd)
            nk_p.append(kp.reshape(bsz, BLOCK, nkv, hd))
            nv_p.append(vp.reshape(bsz, BLOCK, nkv, hd))

            if time_major:
                xs = xs.reshape(t_new, dbsz, d).transpose(1, 0, 2).reshape(dbsz * t_new, d)
                time_major = False
            q, kvt = _rms_matmul(xs, attn_norm[j], wqkv[:, :nq], wqkv[:, nq:].T)
            q3 = q.reshape(dbsz, t_new * nh, hd)
            ck = cache_k[j].transpose(0, 2, 3, 1).reshape(dbsz, nk, win)
            cv = cache_v[j].transpose(0, 2, 3, 1).reshape(dbsz, nk, win)
            o3, nks, nvs = _attn_sample(q3, kvt, ck, cv, attn_sinks[j], t_new=t_new, nh=nh, nkv=nkv, hd=hd,
                                        bb=bb)
            nk_s.append(nks.reshape(dbsz, nkv, hd, win).transpose(0, 3, 1, 2))
            nv_s.append(nvs.reshape(dbsz, nkv, hd, win).transpose(0, 3, 1, 2))
            sample_proj = dict(os=o3.reshape(dbsz * t_new, nq), wo=wo)
        else:
            rec_w = (rec_norm[j], w_rec_in[j].astype(BF16), conv_w[j], conv_b[j],
                     w_rgate[j].astype(BF16), b_rgate[j], w_igate[j].astype(BF16), b_igate[j],
                     lru_lambda[j], w_rec_out[j].astype(BF16))
            xp, cp, hp = _rec_prompt(xp, *rec_w, tm=tm, sub=sub)
            nc_p.append(cp)
            nh_p.append(hp.reshape(bsz, d))

            if not time_major:
                xs = xs.reshape(dbsz, t_new, d).transpose(1, 0, 2).reshape(t_new * dbsz, d)
                time_major = True
            sc = state_conv[j].transpose(1, 0, 2)
            xs, cs, hs = _rec_sample(xs, sc, state_h[j], *rec_w, t_new=t_new)
            nc_s.append(cs.transpose(1, 0, 2))
            nh_s.append(hs)
            sample_proj = {}
        xp, xs = _ffn(xp.reshape(bsz * seq, d), xs, ffn_norm[layer], wfi, wfo, layer, tm=tm_ffn, gf=gf,
                      perm_sub=sub, perm_in=layer % 2 == 1, perm_out=layer % 2 == 0 and layer + 1 < depth,
                      **sample_proj)
        xp = xp.reshape(bsz, seq, d)

    if time_major:
        ys = xs.reshape(t_new, dbsz, d).transpose(1, 0, 2)
    else:
        ys = xs.reshape(dbsz, t_new, d)
    return (xp, ys, jnp.stack(nk_p), jnp.stack(nv_p), jnp.stack(nk_s), jnp.stack(nv_s),
            jnp.stack(nc_p), jnp.stack(nh_p), jnp.stack(nc_s), jnp.stack(nh_s))
```
